```python
import jax, jax.numpy as jnp
from jax import lax
import numpy as np

D_MODEL = 1024
BATCH = 2
SEQ = 16384
DEPTH = 2

N_HEADS = 8
Q_LORA = 256
KV_LORA = 128
QK_NOPE = 64
QK_ROPE = 32
QK_HEAD = QK_NOPE + QK_ROPE
V_HEAD = 64
ATTN_WIDTH = N_HEADS * V_HEAD
ROPE_THETA = 10000.0
Q_BLOCK = 128

SSM_WIDTH = 512
SSM_GROUP = 16
SSM_GROUPS = SSM_WIDTH // SSM_GROUP
SSM_STATE = 64
SSM_CHUNK = 128
DT_MIN = 1e-3
DT_MAX = 1e-1

CONV_WIDTH = 512
CONV_K = 31

D_FF = 2816
N_EXPERTS = 8
TOP_K = 2
D_FF_EXPERT = 3584
MOE_BLOCK = 128
N_DENSE = (DEPTH + 1) // 2
N_MOE = DEPTH // 2

RMS_EPS = 1e-6
LN_EPS = 1e-5

IN_SPLITS = (Q_LORA, KV_LORA, QK_ROPE, SSM_WIDTH, CONV_WIDTH, CONV_WIDTH, 3 * D_MODEL)
D_IN = sum(IN_SPLITS)
IN_OFFSETS = tuple(int(v) for v in np.cumsum(IN_SPLITS)[:-1])

kernel_name = 'hybrid_gated_s5_mla_conformer_moe'


def rmsnorm(x, g):
    xf = x.astype(jnp.float32)
    y = xf * lax.rsqrt(jnp.mean(xf * xf, axis=-1, keepdims=True) + RMS_EPS)
    return (y * g.astype(jnp.float32)).astype(x.dtype)


def layernorm(x, g, b):
    xf = x.astype(jnp.float32)
    mu = jnp.mean(xf, axis=-1, keepdims=True)
    var = jnp.mean(jnp.square(xf - mu), axis=-1, keepdims=True)
    y = (xf - mu) * lax.rsqrt(var + LN_EPS)
    return (y * g.astype(jnp.float32) + b.astype(jnp.float32)).astype(x.dtype)


def rope_tables(positions):
    inv_freq = ROPE_THETA ** (-jnp.arange(0, QK_ROPE, 2, dtype=jnp.float32) / QK_ROPE)
    ang = positions.astype(jnp.float32)[..., None] * inv_freq
    return jnp.cos(ang)[:, :, None, :], jnp.sin(ang)[:, :, None, :]


def apply_rope(t, cos, sin):
    tf = t.astype(jnp.float32)
    half = QK_ROPE // 2
    t1, t2 = tf[..., :half], tf[..., half:]
    return jnp.concatenate([t1 * cos - t2 * sin, t1 * sin + t2 * cos], axis=-1).astype(t.dtype)


def causal_attention(q, k, v):
    B, L, H, _ = q.shape
    nb = L // Q_BLOCK
    scale = QK_HEAD ** -0.5
    qb = q.reshape(B, nb, Q_BLOCK, H, QK_HEAD).transpose(1, 0, 2, 3, 4)
    kpos = jnp.arange(L)
    neg = jnp.finfo(jnp.float32).min

    def one_block(args):
        qi, i = args
        s = jnp.einsum('bqhd,bkhd->bhqk', qi, k).astype(jnp.float32) * scale
        qpos = i * Q_BLOCK + jnp.arange(Q_BLOCK)
        mask = kpos[None, :] <= qpos[:, None]
        p = jax.nn.softmax(jnp.where(mask[None, None], s, neg), axis=-1)
        return jnp.einsum('bhqk,bkhd->bqhd', p.astype(v.dtype), v)

    out = lax.map(one_block, (qb, jnp.arange(nb)))
    return out.transpose(1, 0, 2, 3, 4).reshape(B, L, H * V_HEAD)


def mla_branch(cq, ckv, kpe, cos, sin, q_norm, w_uq, kv_norm, w_ukv):
    B, L, _ = cq.shape
    q = (rmsnorm(cq, q_norm) @ w_uq).reshape(B, L, N_HEADS, QK_HEAD)
    q = jnp.concatenate([q[..., :QK_NOPE], apply_rope(q[..., QK_NOPE:], cos, sin)], axis=-1)
    kv = (rmsnorm(ckv, kv_norm) @ w_ukv).reshape(B, L, N_HEADS, QK_NOPE + V_HEAD)
    k_nope, v = kv[..., :QK_NOPE], kv[..., QK_NOPE:]
    k_pe = apply_rope(kpe[:, :, None, :], cos, sin)
    k = jnp.concatenate([k_nope, jnp.broadcast_to(k_pe, (B, L, N_HEADS, QK_ROPE))], axis=-1)
    return causal_attention(q, k, v)


def _cmul_combine(e1, e2):
    a1r, a1i, b1r, b1i = e1
    a2r, a2i, b2r, b2i = e2
    return (a2r * a1r - a2i * a1i,
            a2r * a1i + a2i * a1r,
            a2r * b1r - a2i * b1i + b2r,
            a2r * b1i + a2i * b1r + b2i)


def s5_branch(u, lam_re, lam_im, log_dt, b_re, b_im, c_re, c_im, d_skip):
    f32 = jnp.float32
    B, L, W = u.shape
    uf = u.astype(f32)
    lr = jnp.minimum(lam_re.astype(f32), -1e-4)
    li = lam_im.astype(f32)
    dt = jnp.exp(log_dt.astype(f32))[:, None]
    mag = jnp.exp(lr * dt)
    a_re, a_im = mag * jnp.cos(li * dt), mag * jnp.sin(li * dt)
    den = lr * lr + li * li
    nr, ni = a_re - 1.0, a_im
    coef_re = (nr * lr + ni * li) / den
    coef_im = (ni * lr - nr * li) / den
    br, bi = b_re.astype(f32), b_im.astype(f32)
    bb_re = coef_re[..., None] * br - coef_im[..., None] * bi
    bb_im = coef_re[..., None] * bi + coef_im[..., None] * br
    cr, ci = c_re.astype(f32), c_im.astype(f32)
    steps = jnp.arange(1, SSM_CHUNK + 1, dtype=f32)[:, None, None]
    pmag = jnp.exp(lr[None] * dt[None] * steps)
    pw_re = pmag * jnp.cos(li[None] * dt[None] * steps)
    pw_im = pmag * jnp.sin(li[None] * dt[None] * steps)
    nc = L // SSM_CHUNK
    uc = uf.reshape(B, nc, SSM_CHUNK, SSM_GROUPS, SSM_GROUP).transpose(1, 0, 2, 3, 4)

    def chunk_step(h, u_c):
        h_re, h_im = h
        bu_re = jnp.einsum('bcgi,gpi->bcgp', u_c, bb_re)
        bu_im = jnp.einsum('bcgi,gpi->bcgp', u_c, bb_im)
        ar = jnp.broadcast_to(a_re, bu_re.shape)
        ai = jnp.broadcast_to(a_im, bu_re.shape)
        _, _, s_re, s_im = lax.associative_scan(_cmul_combine, (ar, ai, bu_re, bu_im), axis=1)
        hr, hi = h_re[:, None], h_im[:, None]
        s_re = s_re + pw_re * hr - pw_im * hi
        s_im = s_im + pw_re * hi + pw_im * hr
        y = jnp.einsum('bcgp,gop->bcgo', s_re, cr) - jnp.einsum('bcgp,gop->bcgo', s_im, ci)
        return (s_re[:, -1], s_im[:, -1]), y

    h0 = jnp.zeros((B, SSM_GROUPS, SSM_STATE), f32)
    _, y = lax.scan(chunk_step, (h0, h0), uc)
    y = y.transpose(1, 0, 2, 3, 4).reshape(B, L, W) + uf * d_skip.astype(f32)
    return y.astype(u.dtype)


def conformer_conv_branch(a, gate, w_dw, b_dw, ln_g, ln_b):
    z = a * jax.nn.sigmoid(gate)
    y = lax.conv_general_dilated(z, w_dw[:, None, :].astype(z.dtype), window_strides=(1,),
                                 padding=((CONV_K - 1, 0),),
                                 dimension_numbers=('NWC', 'WIO', 'NWC'),
                                 feature_group_count=CONV_WIDTH)
    y = layernorm(y + b_dw, ln_g, ln_b)
    return jax.nn.silu(y)


def swiglu(h, w1, w3, w2):
    return (jax.nn.silu(h @ w1) * (h @ w3)) @ w2


def moe_swiglu(h, w_router, w1, w3, w2):
    B, L, D = h.shape
    T = B * L
    xt = h.reshape(T, D)
    logits = (xt @ w_router).astype(jnp.float32)
    top_val, top_idx = lax.top_k(logits, TOP_K)
    gate = jax.nn.softmax(top_val, axis=-1)
    M = T * TOP_K
    e_flat = top_idx.reshape(M)
    g_flat = gate.reshape(M)
    tok_flat = jnp.repeat(jnp.arange(T, dtype=jnp.int32), TOP_K, total_repeat_length=M)
    order = jnp.argsort(e_flat)
    e_sorted = e_flat[order]
    counts = jax.ops.segment_sum(jnp.ones((M,), jnp.int32), e_flat, num_segments=N_EXPERTS)
    starts = jnp.cumsum(counts) - counts
    padded = ((counts + MOE_BLOCK - 1) // MOE_BLOCK) * MOE_BLOCK
    pad_ends = jnp.cumsum(padded)
    pad_starts = pad_ends - padded
    dest = pad_starts[e_sorted] + (jnp.arange(M) - starts[e_sorted])
    n_blocks = -(-M // MOE_BLOCK) + N_EXPERTS
    P = n_blocks * MOE_BLOCK
    tok_buf = jnp.zeros((P,), jnp.int32).at[dest].set(tok_flat[order])
    gate_buf = jnp.zeros((P,), jnp.float32).at[dest].set(g_flat[order])
    block_start = jnp.arange(n_blocks) * MOE_BLOCK
    block_e = jnp.minimum(jnp.searchsorted(pad_ends, block_start, side='right'), N_EXPERTS - 1)
    x_buf = xt[tok_buf].reshape(n_blocks, MOE_BLOCK, D)

    def expert_block(args):
        xb, e = args
        return (jax.nn.silu(xb @ w1[e]) * (xb @ w3[e])) @ w2[e]

    y_buf = lax.map(expert_block, (x_buf, block_e)).reshape(P, D)
    y = jnp.zeros((T, D), h.dtype).at[tok_buf].add(y_buf * gate_buf[:, None].astype(h.dtype))
    return y.reshape(B, L, D)


def setup_inputs(seed: int = 0) -> dict:
    key = jax.random.key(seed)
    ks = iter(jax.random.split(key, 48))
    f32 = jnp.float32

    def nrm(shape, scale):
        return jax.random.normal(next(ks), shape, f32) * scale

    def gain(shape):
        return 1.0 + 0.02 * jax.random.normal(next(ks), shape, f32)

    x = jax.random.normal(next(ks), (BATCH, SEQ, D_MODEL), f32)
    offs = jax.random.randint(next(ks), (BATCH, 1), 0, 1024, dtype=jnp.int32)
    positions = offs + jnp.arange(SEQ, dtype=jnp.int32)[None, :]
    G, P, I = SSM_GROUPS, SSM_STATE, SSM_GROUP
    n_idx = jnp.arange(P, dtype=f32)
    return {
        'x': x,
        'positions': positions,
        'norm_mix': gain((DEPTH, D_MODEL)),
        'w_in': nrm((DEPTH, D_MODEL, D_IN), D_MODEL ** -0.5),
        'q_norm': gain((DEPTH, Q_LORA)),
        'w_uq': nrm((DEPTH, Q_LORA, N_HEADS * QK_HEAD), Q_LORA ** -0.5),
        'kv_norm': gain((DEPTH, KV_LORA)),
        'w_ukv': nrm((DEPTH, KV_LORA, N_HEADS * (QK_NOPE + V_HEAD)), KV_LORA ** -0.5),
        'w_o_attn': nrm((DEPTH, ATTN_WIDTH, D_MODEL), ATTN_WIDTH ** -0.5),
        'ssm_lam_re': -0.5 + nrm((DEPTH, G, P), 0.01),
        'ssm_lam_im': jnp.pi * n_idx + nrm((DEPTH, G, P), 0.01),
        'ssm_log_dt': jax.random.uniform(next(ks), (DEPTH, G), f32, np.log(DT_MIN), np.log(DT_MAX)),
        'ssm_b_re': nrm((DEPTH, G, P, I), (2.0 * I) ** -0.5),
        'ssm_b_im': nrm((DEPTH, G, P, I), (2.0 * I) ** -0.5),
        'ssm_c_re': nrm((DEPTH, G, I, P), (2.0 * P) ** -0.5),
        'ssm_c_im': nrm((DEPTH, G, I, P), (2.0 * P) ** -0.5),
        'ssm_d': nrm((DEPTH, SSM_WIDTH), 1.0),
        'ssm_w_glu': nrm((DEPTH, SSM_WIDTH, SSM_WIDTH), SSM_WIDTH ** -0.5),
        'ssm_b_glu': nrm((DEPTH, SSM_WIDTH), 0.01),
        'w_o_ssm': nrm((DEPTH, SSM_WIDTH, D_MODEL), SSM_WIDTH ** -0.5),
        'conv_w': nrm((DEPTH, CONV_K, CONV_WIDTH), CONV_K ** -0.5),
        'conv_b': nrm((DEPTH, CONV_WIDTH), 0.01),
        'conv_ln_g': gain((DEPTH, CONV_WIDTH)),
        'conv_ln_b': nrm((DEPTH, CONV_WIDTH), 0.01),
        'w_o_conv': nrm((DEPTH, CONV_WIDTH, D_MODEL), CONV_WIDTH ** -0.5),
        'w_out': nrm((DEPTH, D_MODEL, D_MODEL), D_MODEL ** -0.5),
        'norm_ffn': gain((DEPTH, D_MODEL)),
        'ffn_w1': nrm((N_DENSE, D_MODEL, D_FF), D_MODEL ** -0.5),
        'ffn_w3': nrm((N_DENSE, D_MODEL, D_FF), D_MODEL ** -0.5),
        'ffn_w2': nrm((N_DENSE, D_FF, D_MODEL), D_FF ** -0.5),
        'moe_router': nrm((N_MOE, D_MODEL, N_EXPERTS), D_MODEL ** -0.5),
        'moe_w1': nrm((N_MOE, N_EXPERTS, D_MODEL, D_FF_EXPERT), D_MODEL ** -0.5),
        'moe_w3': nrm((N_MOE, N_EXPERTS, D_MODEL, D_FF_EXPERT), D_MODEL ** -0.5),
        'moe_w2': nrm((N_MOE, N_EXPERTS, D_FF_EXPERT, D_MODEL), D_FF_EXPERT ** -0.5),
        'norm_final': gain((D_MODEL,)),
    }


def reference(x, positions, norm_mix, w_in, q_norm, w_uq, kv_norm, w_ukv, w_o_attn,
              ssm_lam_re, ssm_lam_im, ssm_log_dt, ssm_b_re, ssm_b_im, ssm_c_re, ssm_c_im,
              ssm_d, ssm_w_glu, ssm_b_glu, w_o_ssm,
              conv_w, conv_b, conv_ln_g, conv_ln_b, w_o_conv,
              w_out, norm_ffn, ffn_w1, ffn_w3, ffn_w2,
              moe_router, moe_w1, moe_w3, moe_w2, norm_final):
    B, L, D = x.shape
    cos, sin = rope_tables(positions)
    for layer in range(DEPTH):
        h = rmsnorm(x, norm_mix[layer])
        z = h @ w_in[layer]
        cq, ckv, kpe, u_ssm, conv_a, conv_g, gates = jnp.split(z, IN_OFFSETS, axis=-1)
        y_attn = mla_branch(cq, ckv, kpe, cos, sin, q_norm[layer], w_uq[layer],
                            kv_norm[layer], w_ukv[layer]) @ w_o_attn[layer]
        s = s5_branch(u_ssm, ssm_lam_re[layer], ssm_lam_im[layer], ssm_log_dt[layer],
                      ssm_b_re[layer], ssm_b_im[layer], ssm_c_re[layer], ssm_c_im[layer],
                      ssm_d[layer])
        s = jax.nn.gelu(s)
        s = s * jax.nn.sigmoid(s @ ssm_w_glu[layer] + ssm_b_glu[layer])
        y_ssm = s @ w_o_ssm[layer]
        y_conv = conformer_conv_branch(conv_a, conv_g, conv_w[layer], conv_b[layer],
                                       conv_ln_g[layer], conv_ln_b[layer]) @ w_o_conv[layer]
        g = jax.nn.sigmoid(gates).reshape(B, L, 3, D)
        merged = g[:, :, 0] * y_attn + g[:, :, 1] * y_ssm + g[:, :, 2] * y_conv
        x = x + merged @ w_out[layer]
        h = rmsnorm(x, norm_ffn[layer])
        if layer % 2 == 0:
            i = layer // 2
            x = x + swiglu(h, ffn_w1[i], ffn_w3[i], ffn_w2[i])
        else:
            i = layer // 2
            x = x + moe_swiglu(h, moe_router[i], moe_w1[i], moe_w3[i], moe_w2[i])
    return rmsnorm(x, norm_final)
```

```python
import functools

import numpy as np
import jax
import jax.numpy as jnp
from jax import lax
from jax.experimental import pallas as pl
from jax.experimental.pallas import tpu as pltpu

F32 = jnp.float32
BF16 = jnp.bfloat16

D_MODEL = 1024
N_HEADS = 8
Q_LORA = 256
KV_LORA = 128
QK_NOPE = 64
QK_ROPE = 32
QK_HEAD = QK_NOPE + QK_ROPE
V_HEAD = 64
ROPE_THETA = 10000.0
SSM_WIDTH = 512
SSM_GROUP = 16
SSM_GROUPS = SSM_WIDTH // SSM_GROUP
SSM_STATE = 64
SSM_CHUNK = 128
CONV_WIDTH = 512
CONV_K = 31
N_EXPERTS = 8
TOP_K = 2
RMS_EPS = 1e-6
LN_EPS = 1e-5
IN_SPLITS = (Q_LORA, KV_LORA, QK_ROPE, SSM_WIDTH, CONV_WIDTH, CONV_WIDTH, 3 * D_MODEL)
IN_OFFSETS = tuple(int(v) for v in np.cumsum((0,) + IN_SPLITS))

LANES = 128
HEAD_PAD = LANES
VMEM_LIMIT_BYTES = 56 * 1024 * 1024

TOKEN_TILE = 512
ATTN_TILE = 512
MOE_ROW_TILE = 1024
MOE_F_TILE = 512
FFN_F_TILE = 512
ROUTE_TILE = 256
CONV_HALO = 32
NEG_BIG = -1e30


def _cparams(sem):
    return pltpu.CompilerParams(dimension_semantics=sem, vmem_limit_bytes=VMEM_LIMIT_BYTES)


def _const_spec(shape):
    nd = len(shape)
    return pl.BlockSpec(shape, lambda *_: (0,) * nd, pipeline_mode=pl.Buffered(1))


def _rms(x, g):
    return x * lax.rsqrt(jnp.mean(x * x, axis=-1, keepdims=True) + RMS_EPS) * g


def _mm(a, b):
    return jnp.dot(a, b, preferred_element_type=F32)


def _mm_nt(a, b):
    return lax.dot_general(a, b, (((1,), (1,)), ((), ())), preferred_element_type=F32)


def _mm_tn(a, b):
    return lax.dot_general(a, b, (((0,), (0,)), ((), ())), preferred_element_type=F32)


def _rope_tables_kernel(pos_ref, invf_ref, sign_ref, cos_ref, sin_ref):
    ang = pos_ref[...].astype(F32) * invf_ref[...]
    cos_ref[...] = jnp.cos(ang)
    sin_ref[...] = jnp.sin(ang) * sign_ref[...]


def _rope_tables(positions):
    t = positions.size
    tm = min(2048, t)
    half = QK_ROPE // 2
    inv_freq = ROPE_THETA ** (-np.arange(0, QK_ROPE, 2, dtype=np.float32) / QK_ROPE)
    invf = np.zeros((1, LANES), np.float32)
    sign = np.zeros((1, LANES), np.float32)
    invf[0, QK_NOPE:QK_NOPE + half] = inv_freq
    invf[0, QK_NOPE + half:QK_HEAD] = inv_freq
    sign[0, QK_NOPE:QK_NOPE + half] = -1.0
    sign[0, QK_NOPE + half:QK_HEAD] = 1.0
    return pl.pallas_call(
        _rope_tables_kernel,
        grid=(t // tm,),
        in_specs=[pl.BlockSpec((tm, 1), lambda i: (i, 0)),
                  pl.BlockSpec((1, LANES), lambda i: (0, 0)),
                  pl.BlockSpec((1, LANES), lambda i: (0, 0))],
        out_specs=[pl.BlockSpec((tm, LANES), lambda i: (i, 0))] * 2,
        out_shape=[jax.ShapeDtypeStruct((t, LANES), F32)] * 2,
        compiler_params=_cparams(("arbitrary",)),
        name="rope_tables",
    )(positions.reshape(t, 1), jnp.asarray(invf), jnp.asarray(sign))


def _rope_head(xh, cos, sin, lane):
    half = QK_ROPE // 2
    swapped = jnp.where(lane < QK_NOPE + half,
                        pltpu.roll(xh, LANES - half, 1), pltpu.roll(xh, half, 1))
    return xh * cos + swapped * sin


def _mixer_in_kernel(x_ref, nm_ref, wa_ref, wut_ref, wca_ref, wcg_ref, wg_ref, qn_ref, kvn_ref,
                     wq_ref, wkn_ref, wke_ref, wvt_ref, cos_ref, sin_ref,
                     q_ref, k_ref, vt_ref, ut_ref, zc_ref, g_ref):
    tm = x_ref.shape[0]
    hn = _rms(x_ref[...], nm_ref[...]).astype(BF16)
    za = _mm(hn, wa_ref[...])
    cqn = _rms(za[:, :Q_LORA], qn_ref[...]).astype(BF16)
    ckvn = _rms(za[:, Q_LORA:Q_LORA + KV_LORA], kvn_ref[...]).astype(BF16)
    kpe = za[:, Q_LORA + KV_LORA:].astype(BF16)
    q = _mm(cqn, wq_ref[...])
    k = _mm(ckvn, wkn_ref[...]) + _mm(kpe, wke_ref[...])
    cos = cos_ref[...]
    sin = sin_ref[...]
    lane = lax.broadcasted_iota(jnp.int32, (tm, LANES), 1)
    scale = QK_HEAD ** -0.5
    for h in range(N_HEADS):
        sl = slice(h * HEAD_PAD, (h + 1) * HEAD_PAD)
        q_ref[:, sl] = (_rope_head(q[:, sl], cos, sin, lane) * scale).astype(BF16)
        k_ref[:, sl] = _rope_head(k[:, sl], cos, sin, lane).astype(BF16)
    vt = _mm_nt(wvt_ref[...], ckvn)
    ut = _mm_nt(wut_ref[...], hn)
    for c in range(tm // LANES):
        sl = slice(c * LANES, (c + 1) * LANES)
        vt_ref[c] = vt[:, sl].astype(BF16)
        ut_ref[c] = ut[:, sl]
    zc_ref[...] = _mm(hn, wca_ref[...]) * jax.nn.sigmoid(_mm(hn, wcg_ref[...]))
    for c in range(3):
        sl = slice(c * D_MODEL, (c + 1) * D_MODEL)
        g_ref[:, sl] = jax.nn.sigmoid(_mm(hn, wg_ref[:, sl])).astype(BF16)


def _mixer_in(x, nm, w, cos_t, sin_t):
    t = x.shape[0]
    tm = min(TOKEN_TILE, t)
    nch = tm // LANES
    row = lambda c: pl.BlockSpec((tm, c), lambda i: (i, 0))
    chunked = lambda c: pl.BlockSpec((nch, c, LANES), lambda i: (i, 0, 0))
    weights = [nm, w["wa"], w["wut"], w["wca"], w["wcg"], w["wg"], w["qn"], w["kvn"],
               w["wq"], w["wkn"], w["wke"], w["wvt"]]
    return pl.pallas_call(
        _mixer_in_kernel,
        grid=(t // tm,),
        in_specs=[row(D_MODEL)] + [_const_spec(a.shape) for a in weights] + [row(LANES), row(LANES)],
        out_specs=[row(N_HEADS * HEAD_PAD), row(N_HEADS * HEAD_PAD), chunked(N_HEADS * V_HEAD),
                   chunked(SSM_WIDTH), row(CONV_WIDTH), row(3 * D_MODEL)],
        out_shape=[jax.ShapeDtypeStruct((t, N_HEADS * HEAD_PAD), BF16),
                   jax.ShapeDtypeStruct((t, N_HEADS * HEAD_PAD), BF16),
                   jax.ShapeDtypeStruct((t // LANES, N_HEADS * V_HEAD, LANES), BF16),
                   jax.ShapeDtypeStruct((t // LANES, SSM_WIDTH, LANES), F32),
                   jax.ShapeDtypeStruct((t, CONV_WIDTH), F32),
                   jax.ShapeDtypeStruct((t, 3 * D_MODEL), BF16)],
        compiler_params=_cparams(("arbitrary",)),
        name="mixer_in",
    )(x, *weights, cos_t, sin_t)


def _attn_kernel(q_ref, k_ref, vt_ref, o_ref, *, tq, tk):
    qi = pl.program_id(2)
    q = q_ref[...]
    nkc = tk // LANES

    def step(j, carry, masked):
        m, l, acc = carry
        kb = k_ref[pl.ds(pl.multiple_of(j * tk, tk), tk), :]
        st = _mm_nt(kb, q)
        if masked:
            kpos = j * tk + lax.broadcasted_iota(jnp.int32, (tk, tq), 0)
            qpos = qi * tq + lax.broadcasted_iota(jnp.int32, (tk, tq), 1)
            st = jnp.where(kpos <= qpos, st, NEG_BIG)
        m_new = jnp.maximum(m, jnp.max(st, axis=0, keepdims=True))
        alpha = jnp.exp(m - m_new)
        p = jnp.exp(st - m_new)
        l_new = alpha * l + jnp.sum(p, axis=0, keepdims=True)
        vb = jnp.concatenate([vt_ref[j * nkc + c] for c in range(nkc)], axis=1)
        acc_new = alpha * acc + _mm(vb, p.astype(BF16))
        return m_new, l_new, acc_new

    carry = (jnp.full((1, tq), NEG_BIG, F32), jnp.zeros((1, tq), F32), jnp.zeros((V_HEAD, tq), F32))
    n_full = qi * (tq // tk)
    carry = lax.fori_loop(0, n_full, lambda j, c: step(j, c, False), carry)
    for d in range(tq // tk):
        carry = step(n_full + d, carry, True)
    _, l, acc = carry
    out = acc / l
    for c in range(tq // LANES):
        o_ref[c] = out[:, c * LANES:(c + 1) * LANES].astype(BF16)


def _attention(q, k, vt, batch, seq):
    tq = tk = min(ATTN_TILE, seq)
    nq = seq // tq
    kern = functools.partial(_attn_kernel, tq=tq, tk=tk)
    return pl.pallas_call(
        kern,
        grid=(batch, N_HEADS, nq),
        in_specs=[pl.BlockSpec((tq, HEAD_PAD), lambda b, h, i: (b * nq + i, h)),
                  pl.BlockSpec((seq, HEAD_PAD), lambda b, h, i: (b, h)),
                  pl.BlockSpec((seq // LANES, V_HEAD, LANES), lambda b, h, i: (b, h, 0))],
        out_specs=pl.BlockSpec((tq // LANES, V_HEAD, LANES), lambda b, h, i: (b * nq + i, h, 0)),
        out_shape=jax.ShapeDtypeStruct((batch * seq // LANES, N_HEADS * V_HEAD, LANES), BF16),
        compiler_params=_cparams(("arbitrary", "arbitrary", "arbitrary")),
        name="attention",
    )(q, k, vt)


def _ssm_tables_kernel(lrr_ref, lir_ref, lrc_ref, lic_ref, ldt_ref, btr_ref, bti_ref,
                       cr_ref, ci_ref, ctr_ref, cti_ref,
                       klag_ref, win_ref, wout_ref, apr_ref, api_ref):
    ch = SSM_CHUNK
    dt = jnp.exp(ldt_ref[0])

    def power(lr, li, n):
        mag = jnp.exp(lr * dt * n)
        return mag * jnp.cos(li * dt * n), mag * jnp.sin(li * dt * n)

    lr = jnp.minimum(lrr_ref[0], -1e-4)
    li = lir_ref[0]
    a_re, a_im = power(lr, li, 1.0)
    den = lr * lr + li * li
    nr, ni = a_re - 1.0, a_im
    coef_re = (nr * lr + ni * li) / den
    coef_im = (ni * lr - nr * li) / den
    btr, bti = btr_ref[0], bti_ref[0]
    bbr = coef_re * btr - coef_im * bti
    bbi = coef_re * bti + coef_im * btr
    cr, ci = cr_ref[0], ci_ref[0]
    f_re = jnp.concatenate([cr * bbr[i:i + 1] - ci * bbi[i:i + 1] for i in range(SSM_GROUP)], axis=0)
    f_im = jnp.concatenate([cr * bbi[i:i + 1] + ci * bbr[i:i + 1] for i in range(SSM_GROUP)], axis=0)

    lrc = jnp.minimum(lrc_ref[0], -1e-4)
    lic = lic_ref[0]
    lag = lax.broadcasted_iota(jnp.int32, (1, ch), 1).astype(F32)
    pl_re, pl_im = power(lrc, lic, lag)
    hi = lax.Precision.HIGHEST
    klag_ref[0] = (jnp.dot(f_re, pl_re, precision=hi, preferred_element_type=F32)
                   - jnp.dot(f_im, pl_im, precision=hi, preferred_element_type=F32))

    back = (ch - 1) - lax.broadcasted_iota(jnp.int32, (ch, 1), 0).astype(F32)
    q_re, q_im = power(lr, li, back)
    for i in range(SSM_GROUP):
        w_re = q_re * bbr[i:i + 1] - q_im * bbi[i:i + 1]
        w_im = q_re * bbi[i:i + 1] + q_im * bbr[i:i + 1]
        win_ref[0, i * ch:(i + 1) * ch, :] = jnp.concatenate([w_re, w_im], axis=1).astype(BF16)

    pw_re, pw_im = power(lrc, lic, lag + 1.0)
    ctr, cti = ctr_ref[0], cti_ref[0]
    p = SSM_STATE
    for o in range(SSM_GROUP):
        c_re, c_im = ctr[:, o:o + 1], cti[:, o:o + 1]
        wout_ref[0, :p, o * ch:(o + 1) * ch] = (c_re * pw_re - c_im * pw_im).astype(BF16)
        wout_ref[0, p:, o * ch:(o + 1) * ch] = (-(c_re * pw_im + c_im * pw_re)).astype(BF16)

    apr_ref[0], api_ref[0] = power(lr, li, float(ch))


def _ssm_tables(lam_re, lam_im, log_dt, b_re, b_im, c_re, c_im):
    n = lam_re.shape[0]
    p, g, ch = SSM_STATE, SSM_GROUP, SSM_CHUNK
    args = [lam_re.reshape(n, 1, p), lam_im.reshape(n, 1, p),
            lam_re.reshape(n, p, 1), lam_im.reshape(n, p, 1), log_dt.reshape(n, 1, 1),
            jnp.swapaxes(b_re, 1, 2), jnp.swapaxes(b_im, 1, 2), c_re, c_im,
            jnp.swapaxes(c_re, 1, 2), jnp.swapaxes(c_im, 1, 2)]
    spec = lambda a: pl.BlockSpec((1,) + a.shape[1:], lambda i: (i, 0, 0))
    out_shape = [jax.ShapeDtypeStruct((n, g * g, ch), F32),
                 jax.ShapeDtypeStruct((n, g * ch, 2 * p), BF16),
                 jax.ShapeDtypeStruct((n, 2 * p, g * ch), BF16),
                 jax.ShapeDtypeStruct((n, 1, p), F32),
                 jax.ShapeDtypeStruct((n, 1, p), F32)]
    return pl.pallas_call(
        _ssm_tables_kernel,
        grid=(n,),
        in_specs=[spec(a) for a in args],
        out_specs=[spec(s) for s in out_shape],
        out_shape=out_shape,
        compiler_params=_cparams(("arbitrary",)),
        name="ssm_tables",
    )(*args)


def _ssm_kernel(u_ref, klag_ref, win_ref, wout_ref, apr_ref, api_ref, d_ref, s_ref,
                mt_ref, hin_re, hin_im, hst_re, hst_im, *, batch):
    ch, g, p = SSM_CHUNK, SSM_GROUP, SSM_STATE
    nc = u_ref.shape[0]
    ncb = nc // batch
    causal = (lax.broadcasted_iota(jnp.int32, (ch, ch), 1)
              >= lax.broadcasted_iota(jnp.int32, (ch, ch), 0))

    def build(i, _):
        for o in range(g):
            kv = klag_ref[0, pl.ds(i * g + o, 1), :]
            tz = pltpu.roll(jnp.broadcast_to(kv, (ch, ch)), 0, 1, stride=1, stride_axis=0)
            mt_ref[pl.ds(pl.multiple_of(i * ch, ch), ch), o * ch:(o + 1) * ch] = (
                jnp.where(causal, tz, 0.0).astype(BF16))
        return 0

    lax.fori_loop(0, g, build, 0)

    lhs = jnp.concatenate([u_ref[:, i, :].astype(BF16) for i in range(g)], axis=1)
    y = _mm(lhs, mt_ref[...])
    hin = _mm(lhs, win_ref[0])
    hin_re[...] = hin[:, :p]
    hin_im[...] = hin[:, p:]
    a_re, a_im = apr_ref[0], api_ref[0]

    def scan(c, carry):
        new = []
        for b in range(batch):
            h_re, h_im = carry[b]
            r = b * ncb + c
            hst_re[pl.ds(r, 1), :] = h_re
            hst_im[pl.ds(r, 1), :] = h_im
            x_re = hin_re[pl.ds(r, 1), :]
            x_im = hin_im[pl.ds(r, 1), :]
            new.append((a_re * h_re - a_im * h_im + x_re, a_re * h_im + a_im * h_re + x_im))
        return tuple(new)

    zero = jnp.zeros((1, p), F32)
    lax.fori_loop(0, ncb, scan, tuple((zero, zero) for _ in range(batch)))
    y = y + _mm(hst_re[...].astype(BF16), wout_ref[0, :p, :]) + _mm(hst_im[...].astype(BF16), wout_ref[0, p:, :])
    for o in range(g):
        yo = y[:, o * ch:(o + 1) * ch] + u_ref[:, o, :] * d_ref[0, o:o + 1, :]
        s_ref[:, o, :] = jax.nn.gelu(yo)


def _ssm(ut, klag, win, wout, apr, api, d_skip, layer, batch):
    nc = ut.shape[0]
    g, ch, p = SSM_GROUP, SSM_CHUNK, SSM_STATE
    base = layer * SSM_GROUPS
    tab = lambda a: pl.BlockSpec((1,) + a.shape[1:], lambda i: (base + i, 0, 0))
    kern = functools.partial(_ssm_kernel, batch=batch)
    return pl.pallas_call(
        kern,
        grid=(SSM_GROUPS,),
        in_specs=[pl.BlockSpec((nc, g, LANES), lambda i: (0, i, 0)),
                  tab(klag), tab(win), tab(wout), tab(apr), tab(api),
                  pl.BlockSpec((1, g, 1), lambda i: (i, 0, 0))],
        out_specs=pl.BlockSpec((nc, g, LANES), lambda i: (0, i, 0)),
        out_shape=jax.ShapeDtypeStruct(ut.shape, F32),
        scratch_shapes=[pltpu.VMEM((g * ch, g * ch), BF16),
                        pltpu.VMEM((nc, p), F32), pltpu.VMEM((nc, p), F32),
                        pltpu.VMEM((nc, p), F32), pltpu.VMEM((nc, p), F32)],
        compiler_params=_cparams(("arbitrary",)),
        name="ssm",
    )(ut, klag, win, wout, apr, api, d_skip.reshape(SSM_GROUPS, g, 1))


def _merge_kernel(yt_ref, st_ref, zc_ref, halo_ref, g_ref, x_ref,
                  woa_ref, wglut_ref, bglu_ref, wos_ref, cw_ref, cb_ref, lng_ref, lnb_ref,
                  woc_ref, wout_ref, nf_ref, *rest, tiles_per_seq, with_router):
    if with_router:
        wr_ref, xn_ref, h2_ref, route_ref, zs_ref = rest
    else:
        xn_ref, h2_ref, zs_ref = rest
    tm = x_ref.shape[0]
    nch = tm // LANES
    i = pl.program_id(0)

    yt = jnp.concatenate([yt_ref[c] for c in range(nch)], axis=1)
    y_attn = _mm_tn(yt, woa_ref[...])

    st = jnp.concatenate([st_ref[c] for c in range(nch)], axis=1)
    gate = jax.nn.sigmoid(_mm(wglut_ref[...], st.astype(BF16)) + bglu_ref[...])
    y_ssm = _mm_tn((st * gate).astype(BF16), wos_ref[...])

    first = (i % tiles_per_seq) == 0
    zs_ref[:CONV_HALO, :] = jnp.where(first, 0.0, halo_ref[...])
    zs_ref[CONV_HALO:, :] = zc_ref[...]
    off = CONV_HALO - (CONV_K - 1)
    conv = jnp.zeros((tm, CONV_WIDTH), F32)
    for kk in range(CONV_K):
        conv = conv + zs_ref[pl.ds(off + kk, tm), :] * cw_ref[kk:kk + 1, :]
    conv = conv + cb_ref[...]
    mu = jnp.mean(conv, axis=-1, keepdims=True)
    cen = conv - mu
    var = jnp.mean(cen * cen, axis=-1, keepdims=True)
    yc = jax.nn.silu(cen * lax.rsqrt(var + LN_EPS) * lng_ref[...] + lnb_ref[...])
    y_conv = _mm(yc.astype(BF16), woc_ref[...])

    d = D_MODEL
    merged = (g_ref[:, :d].astype(F32) * y_attn + g_ref[:, d:2 * d].astype(F32) * y_ssm
              + g_ref[:, 2 * d:].astype(F32) * y_conv)
    xn = x_ref[...] + _mm(merged.astype(BF16), wout_ref[...])
    xn_ref[...] = xn
    h = _rms(xn, nf_ref[...])
    h2_ref[...] = h.astype(h2_ref.dtype)

    if with_router:
        logits = jnp.dot(h, wr_ref[...], precision=lax.Precision.HIGHEST, preferred_element_type=F32)
        lane = lax.broadcasted_iota(jnp.int32, (tm, LANES), 1)
        logits = jnp.where(lane < N_EXPERTS, logits, NEG_BIG)
        m1 = jnp.max(logits, axis=-1, keepdims=True)
        i1 = jnp.min(jnp.where(logits == m1, lane, LANES), axis=-1, keepdims=True)
        rest_l = jnp.where(lane == i1, NEG_BIG, logits)
        m2 = jnp.max(rest_l, axis=-1, keepdims=True)
        i2 = jnp.min(jnp.where(rest_l == m2, lane, LANES), axis=-1, keepdims=True)
        e = jnp.exp(m2 - m1)
        g1 = 1.0 / (1.0 + e)
        g2 = e / (1.0 + e)
        route_ref[...] = jnp.where(lane == 0, i1.astype(F32),
                         jnp.where(lane == 1, i2.astype(F32),
                         jnp.where(lane == 2, g1, jnp.where(lane == 3, g2, 0.0))))


def _merge(yt, st, zc, gates, x, w, seq, with_router):
    t = x.shape[0]
    tm = min(TOKEN_TILE, t, seq)
    nch = tm // LANES
    hpt = tm // CONV_HALO
    row = lambda c: pl.BlockSpec((tm, c), lambda i: (i, 0))
    chunked = lambda c: pl.BlockSpec((nch, c, LANES), lambda i: (i, 0, 0))
    weights = [w["woa"], w["wglut"], w["bglu"], w["wos"], w["cw"], w["cb"], w["lng"], w["lnb"],
               w["woc"], w["wout"], w["nf"]]
    out_shape = [jax.ShapeDtypeStruct((t, D_MODEL), F32),
                 jax.ShapeDtypeStruct((t, D_MODEL), F32 if with_router else BF16)]
    out_specs = [row(D_MODEL), row(D_MODEL)]
    if with_router:
        weights.append(w["wr"])
        out_shape.append(jax.ShapeDtypeStruct((t, LANES), F32))
        out_specs.append(row(LANES))
    kern = functools.partial(_merge_kernel, tiles_per_seq=seq // tm, with_router=with_router)
    return pl.pallas_call(
        kern,
        grid=(t // tm,),
        in_specs=[chunked(N_HEADS * V_HEAD), chunked(SSM_WIDTH), row(CONV_WIDTH),
                  pl.BlockSpec((CONV_HALO, CONV_WIDTH), lambda i: (jnp.maximum(i * hpt - 1, 0), 0)),
                  row(3 * D_MODEL), row(D_MODEL)] + [_const_spec(a.shape) for a in weights],
        out_specs=out_specs,
        out_shape=out_shape,
        scratch_shapes=[pltpu.VMEM((CONV_HALO + tm, CONV_WIDTH), F32)],
        compiler_params=_cparams(("arbitrary",)),
        name="merge_router" if with_router else "merge",
    )(yt, st, zc, zc, gates, x, *weights)


def _ffn_kernel(x_ref, h_ref, w1_ref, w3_ref, w2_ref, o_ref):
    h = h_ref[...]
    f = w1_ref.shape[1]
    acc = x_ref[...]
    for lo in range(0, f, FFN_F_TILE):
        hi = min(lo + FFN_F_TILE, f)
        a = _mm(h, w1_ref[:, lo:hi])
        b = _mm(h, w3_ref[:, lo:hi])
        acc = acc + _mm((jax.nn.silu(a) * b).astype(BF16), w2_ref[lo:hi, :])
    o_ref[...] = acc


def _ffn(x, h, w1, w3, w2):
    t = x.shape[0]
    tm = min(TOKEN_TILE, t)
    row = pl.BlockSpec((tm, D_MODEL), lambda i: (i, 0))
    return pl.pallas_call(
        _ffn_kernel,
        grid=(t // tm,),
        in_specs=[row, row, _const_spec(w1.shape), _const_spec(w3.shape), _const_spec(w2.shape)],
        out_specs=row,
        out_shape=jax.ShapeDtypeStruct((t, D_MODEL), F32),
        compiler_params=_cparams(("arbitrary",)),
        name="ffn",
    )(x, h, w1, w3, w2)


def _dispatch_kernel(dest_ref, h_ref, xs_in_ref, xs_ref, sem):
    del xs_in_ref
    n = dest_ref.shape[2]

    def row_copy(r):
        return pltpu.make_async_copy(h_ref.at[pl.ds(r // TOP_K, 1), :],
                                     xs_ref.at[pl.ds(dest_ref[0, 0, r], 1), :], sem)

    def issue(r, c):
        row_copy(r).start()
        return c

    def drain(r, c):
        row_copy(r).wait()
        return c

    lax.fori_loop(0, n, issue, 0)
    lax.fori_loop(0, n, drain, 0)


def _dispatch(h, dest, rows):
    t = h.shape[0]
    tm = min(ROUTE_TILE, t)
    nt = t // tm
    xs0 = jnp.zeros((rows, D_MODEL), h.dtype)
    return pl.pallas_call(
        _dispatch_kernel,
        grid=(nt,),
        in_specs=[pl.BlockSpec((1, 1, TOP_K * tm), lambda i: (i, 0, 0), memory_space=pltpu.SMEM),
                  pl.BlockSpec((tm, D_MODEL), lambda i: (i, 0)),
                  pl.BlockSpec(memory_space=pl.ANY)],
        out_specs=pl.BlockSpec(memory_space=pl.ANY),
        out_shape=jax.ShapeDtypeStruct((rows, D_MODEL), h.dtype),
        scratch_shapes=[pltpu.SemaphoreType.DMA(())],
        input_output_aliases={2: 0},
        compiler_params=_cparams(("arbitrary",)),
        name="moe_dispatch",
    )(dest.reshape(nt, 1, TOP_K * tm), h, xs0)


def _experts_kernel(te_ref, nu_ref, x_ref, w1_ref, w3_ref, w2_ref, y_ref, xb_ref):
    i = pl.program_id(0)
    j = pl.program_id(1)
    used = i < nu_ref[0]

    @pl.when(jnp.logical_and(used, j == 0))
    def _():
        xb_ref[...] = x_ref[...].astype(BF16)

    @pl.when(jnp.logical_and(jnp.logical_not(used), j == 0))
    def _():
        y_ref[...] = jnp.zeros_like(y_ref)

    @pl.when(used)
    def _():
        xb = xb_ref[...]
        a = _mm(xb, w1_ref[0])
        b = _mm(xb, w3_ref[0])
        part = _mm((jax.nn.silu(a) * b).astype(BF16), w2_ref[0])

        @pl.when(j == 0)
        def _():
            y_ref[...] = part

        @pl.when(j > 0)
        def _():
            y_ref[...] += part


def _experts(xs, tile_expert, n_used, w1, w3, w2):
    rows = xs.shape[0]
    tm = MOE_ROW_TILE
    f = w1.shape[2]
    tf = min(MOE_F_TILE, f)
    nf = f // tf

    def fchunk(i, j, nu):
        return jnp.where(i < nu[0], j, nf - 1)

    grid_spec = pltpu.PrefetchScalarGridSpec(
        num_scalar_prefetch=2,
        grid=(rows // tm, nf),
        in_specs=[pl.BlockSpec((tm, D_MODEL), lambda i, j, te, nu: (i, 0)),
                  pl.BlockSpec((1, D_MODEL, tf), lambda i, j, te, nu: (te[i], 0, fchunk(i, j, nu))),
                  pl.BlockSpec((1, D_MODEL, tf), lambda i, j, te, nu: (te[i], 0, fchunk(i, j, nu))),
                  pl.BlockSpec((1, tf, D_MODEL), lambda i, j, te, nu: (te[i], fchunk(i, j, nu), 0))],
        out_specs=pl.BlockSpec((tm, D_MODEL), lambda i, j, te, nu: (i, 0)),
        scratch_shapes=[pltpu.VMEM((tm, D_MODEL), BF16)],
    )
    return pl.pallas_call(
        _experts_kernel,
        grid_spec=grid_spec,
        out_shape=jax.ShapeDtypeStruct((rows, D_MODEL), F32),
        compiler_params=_cparams(("arbitrary", "arbitrary")),
        name="moe_experts",
    )(tile_expert, n_used, xs, w1, w3, w2)


def _combine_kernel(dest_ref, x_ref, route_ref, nfin_ref, ys_ref, o_ref, buf_ref, sem):
    tm = x_ref.shape[0]
    n = TOP_K * tm

    def row_copy(r):
        slot = (r % TOP_K) * tm + r // TOP_K
        return pltpu.make_async_copy(ys_ref.at[pl.ds(dest_ref[0, 0, r], 1), :],
                                     buf_ref.at[pl.ds(slot, 1), :], sem)

    def issue(r, c):
        row_copy(r).start()
        return c

    def drain(r, c):
        row_copy(r).wait()
        return c

    lax.fori_loop(0, n, issue, 0)
    lax.fori_loop(0, n, drain, 0)
    route = route_ref[...]
    y = x_ref[...] + route[:, 2:3] * buf_ref[:tm, :] + route[:, 3:4] * buf_ref[tm:, :]
    o_ref[...] = _rms(y, nfin_ref[...])


def _combine(x, route, dest, ys, norm_final):
    t = x.shape[0]
    tm = min(ROUTE_TILE, t)
    nt = t // tm
    return pl.pallas_call(
        _combine_kernel,
        grid=(nt,),
        in_specs=[pl.BlockSpec((1, 1, TOP_K * tm), lambda i: (i, 0, 0), memory_space=pltpu.SMEM),
                  pl.BlockSpec((tm, D_MODEL), lambda i: (i, 0)),
                  pl.BlockSpec((tm, LANES), lambda i: (i, 0)),
                  _const_spec((1, D_MODEL)),
                  pl.BlockSpec(memory_space=pl.ANY)],
        out_specs=pl.BlockSpec((tm, D_MODEL), lambda i: (i, 0)),
        out_shape=jax.ShapeDtypeStruct((t, D_MODEL), F32),
        scratch_shapes=[pltpu.VMEM((TOP_K * tm, D_MODEL), F32), pltpu.SemaphoreType.DMA(())],
        compiler_params=_cparams(("arbitrary",)),
        name="moe_combine",
    )(dest.reshape(nt, 1, TOP_K * tm), x, route, norm_final.reshape(1, D_MODEL), ys)


def _moe_plan(route, row_tile):
    t = route.shape[0]
    m = t * TOP_K
    e_flat = route[:, :TOP_K].astype(jnp.int32).reshape(m)
    onehot = (e_flat[:, None] == jnp.arange(N_EXPERTS, dtype=jnp.int32)[None, :]).astype(jnp.int32)
    csum = jnp.cumsum(onehot, axis=0)
    rank = jnp.sum((csum - onehot) * onehot, axis=1)
    counts = csum[-1]
    padded = ((counts + row_tile - 1) // row_tile) * row_tile
    pad_ends = jnp.cumsum(padded)
    pad_starts = pad_ends - padded
    dest = jnp.sum(onehot * pad_starts[None, :], axis=1) + rank
    n_tiles = -(-m // row_tile) + N_EXPERTS
    tile_start = jnp.arange(n_tiles, dtype=jnp.int32) * row_tile
    tile_expert = jnp.minimum(jnp.searchsorted(pad_ends, tile_start, side="right"), N_EXPERTS - 1)
    n_used = (pad_ends[-1] // row_tile).reshape(1)
    return dest.astype(jnp.int32), tile_expert.astype(jnp.int32), n_used.astype(jnp.int32), n_tiles * row_tile


def _prep_mixer_weights(l, w_in, q_norm, w_uq, kv_norm, w_ukv, w_o_attn, ssm_w_glu, ssm_b_glu, w_o_ssm,
                        conv_w, conv_b, conv_ln_g, conv_ln_b, w_o_conv, w_out, norm_ffn):
    o = IN_OFFSETS
    wi = w_in[l]
    low = wi[:, o[0]:o[3]]
    wa = jnp.pad(low, ((0, 0), (0, 4 * LANES - low.shape[1]))).astype(BF16)
    wq = jnp.pad(w_uq[l].reshape(Q_LORA, N_HEADS, QK_HEAD),
                 ((0, 0), (0, 0), (0, HEAD_PAD - QK_HEAD))).reshape(Q_LORA, N_HEADS * HEAD_PAD).astype(BF16)
    wkv = w_ukv[l].reshape(KV_LORA, N_HEADS, QK_NOPE + V_HEAD)
    wkn = jnp.pad(wkv[..., :QK_NOPE], ((0, 0), (0, 0), (0, HEAD_PAD - QK_NOPE))
                  ).reshape(KV_LORA, N_HEADS * HEAD_PAD).astype(BF16)
    wvt = wkv[..., QK_NOPE:].reshape(KV_LORA, N_HEADS * V_HEAD).T.astype(BF16)
    place = np.zeros((LANES, N_HEADS * HEAD_PAD), np.float32)
    for h in range(N_HEADS):
        for r in range(QK_ROPE):
            place[r, h * HEAD_PAD + QK_NOPE + r] = 1.0
    return dict(
        wa=wa, wut=wi[:, o[3]:o[4]].T.astype(BF16), wca=wi[:, o[4]:o[5]].astype(BF16),
        wcg=wi[:, o[5]:o[6]].astype(BF16), wg=wi[:, o[6]:o[7]].astype(BF16),
        qn=q_norm[l].reshape(1, Q_LORA), kvn=kv_norm[l].reshape(1, KV_LORA),
        wq=wq, wkn=wkn, wke=jnp.asarray(place, BF16), wvt=wvt,
        woa=w_o_attn[l].astype(BF16), wglut=ssm_w_glu[l].T.astype(BF16),
        bglu=ssm_b_glu[l].reshape(SSM_WIDTH, 1), wos=w_o_ssm[l].astype(BF16),
        cw=conv_w[l], cb=conv_b[l].reshape(1, CONV_WIDTH), lng=conv_ln_g[l].reshape(1, CONV_WIDTH),
        lnb=conv_ln_b[l].reshape(1, CONV_WIDTH), woc=w_o_conv[l].astype(BF16),
        wout=w_out[l].astype(BF16), nf=norm_ffn[l].reshape(1, D_MODEL))


def kernel(x, positions, norm_mix, w_in, q_norm, w_uq, kv_norm, w_ukv, w_o_attn, ssm_lam_re, ssm_lam_im, ssm_log_dt, ssm_b_re, ssm_b_im, ssm_c_re, ssm_c_im, ssm_d, ssm_w_glu, ssm_b_glu, w_o_ssm, conv_w, conv_b, conv_ln_g, conv_ln_b, w_o_conv, w_out, norm_ffn, ffn_w1, ffn_w3, ffn_w2, moe_router, moe_w1, moe_w3, moe_w2, norm_final):
    batch, seq, d = x.shape
    depth = w_in.shape[0]
    assert d == D_MODEL and depth == 2 and seq % SSM_CHUNK == 0
    t = batch * seq
    xf = x.reshape(t, d)
    cos_t, sin_t = _rope_tables(positions)
    ng = depth * SSM_GROUPS
    klag, win, wout_s, apr, api = _ssm_tables(
        ssm_lam_re.reshape(ng, SSM_STATE), ssm_lam_im.reshape(ng, SSM_STATE), ssm_log_dt.reshape(ng),
        ssm_b_re.reshape(ng, SSM_STATE, SSM_GROUP), ssm_b_im.reshape(ng, SSM_STATE, SSM_GROUP),
        ssm_c_re.reshape(ng, SSM_GROUP, SSM_STATE), ssm_c_im.reshape(ng, SSM_GROUP, SSM_STATE))

    for layer in range(depth):
        moe_layer = layer % 2 == 1
        w = _prep_mixer_weights(layer, w_in, q_norm, w_uq, kv_norm, w_ukv, w_o_attn, ssm_w_glu, ssm_b_glu,
                                w_o_ssm, conv_w, conv_b, conv_ln_g, conv_ln_b, w_o_conv, w_out, norm_ffn)
        q, k, vt, ut, zc, gates = _mixer_in(xf, norm_mix[layer].reshape(1, d), w, cos_t, sin_t)
        yt = _attention(q, k, vt, batch, seq)
        st = _ssm(ut, klag, win, wout_s, apr, api, ssm_d[layer], layer, batch)
        i = layer // 2
        if not moe_layer:
            xn, h2 = _merge(yt, st, zc, gates, xf, w, seq, with_router=False)
            xf = _ffn(xn, h2, ffn_w1[i].astype(BF16), ffn_w3[i].astype(BF16), ffn_w2[i].astype(BF16))
        else:
            w["wr"] = jnp.pad(moe_router[i], ((0, 0), (0, LANES - N_EXPERTS)))
            xn, h2, route = _merge(yt, st, zc, gates, xf, w, seq, with_router=True)
            dest, tile_expert, n_used, rows = _moe_plan(route, MOE_ROW_TILE)
            xs = _dispatch(h2, dest, rows)
            ys = _experts(xs, tile_expert, n_used, moe_w1[i].astype(BF16), moe_w3[i].astype(BF16),
                          moe_w2[i].astype(BF16))
            xf = _combine(xn, route, dest, ys, norm_final)
    return xf.reshape(batch, seq, d)
```

```python
import functools

import numpy as np
import jax
import jax.numpy as jnp
from jax import lax
from jax.experimental import pallas as pl
from jax.experimental.pallas import tpu as pltpu

F32 = jnp.float32
BF16 = jnp.bfloat16

D_MODEL = 1024
N_HEADS = 8
Q_LORA = 256
KV_LORA = 128
QK_NOPE = 64
QK_ROPE = 32
QK_HEAD = QK_NOPE + QK_ROPE
V_HEAD = 64
ROPE_THETA = 10000.0
SSM_WIDTH = 512
SSM_GROUP = 16
SSM_GROUPS = SSM_WIDTH // SSM_GROUP
SSM_STATE = 64
SSM_CHUNK = 128
CONV_WIDTH = 512
CONV_K = 31
N_EXPERTS = 8
TOP_K = 2
RMS_EPS = 1e-6
LN_EPS = 1e-5
IN_SPLITS = (Q_LORA, KV_LORA, QK_ROPE, SSM_WIDTH, CONV_WIDTH, CONV_WIDTH, 3 * D_MODEL)
IN_OFFSETS = tuple(int(v) for v in np.cumsum((0,) + IN_SPLITS))

LANES = 128
HEAD_PAD = LANES
VMEM_LIMIT_BYTES = 56 * 1024 * 1024

TOKEN_TILE = 512
ATTN_TILE = 1024
ATTN_COLS = 256
ATTN_ROWS = 256
MOE_ROW_TILE = 1024
MOE_F_TILE = 512
FFN_F_TILE = 512
ROUTE_TILE = 256
CONV_HALO = 32
SUM_ROWS = 16
NEG_BIG = -1e30
LOG2_E = float(np.log2(np.e))


def _cparams(sem):
    return pltpu.CompilerParams(dimension_semantics=sem, vmem_limit_bytes=VMEM_LIMIT_BYTES)


def _const_spec(shape):
    nd = len(shape)
    return pl.BlockSpec(shape, lambda *_: (0,) * nd, pipeline_mode=pl.Buffered(1))


def _rms(x, g):
    return x * lax.rsqrt(jnp.mean(x * x, axis=-1, keepdims=True) + RMS_EPS) * g


def _mm(a, b):
    return jnp.dot(a, b, preferred_element_type=F32)


def _mm_nt(a, b):
    return lax.dot_general(a, b, (((1,), (1,)), ((), ())), preferred_element_type=F32)


def _mm_tn(a, b):
    return lax.dot_general(a, b, (((0,), (0,)), ((), ())), preferred_element_type=F32)


def _rope_tables_kernel(pos_ref, invf_ref, sign_ref, cos_ref, sin_ref):
    ang = pos_ref[...].astype(F32) * invf_ref[...]
    cos_ref[...] = jnp.cos(ang)
    sin_ref[...] = jnp.sin(ang) * sign_ref[...]


def _rope_tables(positions):
    t = positions.size
    tm = min(2048, t)
    half = QK_ROPE // 2
    inv_freq = ROPE_THETA ** (-np.arange(0, QK_ROPE, 2, dtype=np.float32) / QK_ROPE)
    invf = np.zeros((1, LANES), np.float32)
    sign = np.zeros((1, LANES), np.float32)
    invf[0, QK_NOPE:QK_NOPE + half] = inv_freq
    invf[0, QK_NOPE + half:QK_HEAD] = inv_freq
    sign[0, QK_NOPE:QK_NOPE + half] = -1.0
    sign[0, QK_NOPE + half:QK_HEAD] = 1.0
    return pl.pallas_call(
        _rope_tables_kernel,
        grid=(t // tm,),
        in_specs=[pl.BlockSpec((tm, 1), lambda i: (i, 0)),
                  pl.BlockSpec((1, LANES), lambda i: (0, 0)),
                  pl.BlockSpec((1, LANES), lambda i: (0, 0))],
        out_specs=[pl.BlockSpec((tm, LANES), lambda i: (i, 0))] * 2,
        out_shape=[jax.ShapeDtypeStruct((t, LANES), F32)] * 2,
        compiler_params=_cparams(("arbitrary",)),
        name="rope_tables",
    )(positions.reshape(t, 1), jnp.asarray(invf), jnp.asarray(sign))


def _rope_head(xh, cos, sin, lane):
    half = QK_ROPE // 2
    swapped = jnp.where(lane < QK_NOPE + half,
                        pltpu.roll(xh, LANES - half, 1), pltpu.roll(xh, half, 1))
    return xh * cos + swapped * sin


def _mixer_in_kernel(x_ref, nm_ref, wa_ref, wut_ref, wca_ref, wcg_ref, wg_ref, qn_ref, kvn_ref,
                     wq_ref, wkn_ref, wke_ref, wvt_ref, cos_ref, sin_ref,
                     q_ref, k_ref, vt_ref, ut_ref, zc_ref, g_ref):
    tm = x_ref.shape[0]
    hn = _rms(x_ref[...], nm_ref[...]).astype(BF16)
    za = _mm(hn, wa_ref[...])
    cqn = _rms(za[:, :Q_LORA], qn_ref[...]).astype(BF16)
    ckvn = _rms(za[:, Q_LORA:Q_LORA + KV_LORA], kvn_ref[...]).astype(BF16)
    kpe = za[:, Q_LORA + KV_LORA:].astype(BF16)
    q = _mm(cqn, wq_ref[...])
    k = _mm(ckvn, wkn_ref[...]) + _mm(kpe, wke_ref[...])
    cos = cos_ref[...]
    sin = sin_ref[...]
    lane = lax.broadcasted_iota(jnp.int32, (tm, LANES), 1)
    scale = QK_HEAD ** -0.5 * LOG2_E
    for h in range(N_HEADS):
        sl = slice(h * HEAD_PAD, (h + 1) * HEAD_PAD)
        q_ref[:, sl] = (_rope_head(q[:, sl], cos, sin, lane) * scale).astype(BF16)
        k_ref[:, sl] = _rope_head(k[:, sl], cos, sin, lane).astype(BF16)
    vt = _mm_nt(wvt_ref[...], ckvn)
    ut = _mm_nt(wut_ref[...], hn)
    for c in range(tm // LANES):
        sl = slice(c * LANES, (c + 1) * LANES)
        vt_ref[c] = vt[:, sl].astype(BF16)
        ut_ref[c] = ut[:, sl]
    zc_ref[...] = _mm(hn, wca_ref[...]) * jax.nn.sigmoid(_mm(hn, wcg_ref[...]))
    for c in range(3):
        sl = slice(c * D_MODEL, (c + 1) * D_MODEL)
        g_ref[:, sl] = jax.nn.sigmoid(_mm(hn, wg_ref[:, sl])).astype(BF16)


def _mixer_in(x, nm, w, cos_t, sin_t):
    t = x.shape[0]
    tm = min(TOKEN_TILE, t)
    nch = tm // LANES
    row = lambda c: pl.BlockSpec((tm, c), lambda i: (i, 0))
    chunked = lambda c: pl.BlockSpec((nch, c, LANES), lambda i: (i, 0, 0))
    weights = [nm, w["wa"], w["wut"], w["wca"], w["wcg"], w["wg"], w["qn"], w["kvn"],
               w["wq"], w["wkn"], w["wke"], w["wvt"]]
    return pl.pallas_call(
        _mixer_in_kernel,
        grid=(t // tm,),
        in_specs=[row(D_MODEL)] + [_const_spec(a.shape) for a in weights] + [row(LANES), row(LANES)],
        out_specs=[row(N_HEADS * HEAD_PAD), row(N_HEADS * HEAD_PAD), chunked(N_HEADS * V_HEAD),
                   chunked(SSM_WIDTH), row(CONV_WIDTH), row(3 * D_MODEL)],
        out_shape=[jax.ShapeDtypeStruct((t, N_HEADS * HEAD_PAD), BF16),
                   jax.ShapeDtypeStruct((t, N_HEADS * HEAD_PAD), BF16),
                   jax.ShapeDtypeStruct((t // LANES, N_HEADS * V_HEAD, LANES), BF16),
                   jax.ShapeDtypeStruct((t // LANES, SSM_WIDTH, LANES), F32),
                   jax.ShapeDtypeStruct((t, CONV_WIDTH), F32),
                   jax.ShapeDtypeStruct((t, 3 * D_MODEL), BF16)],
        compiler_params=_cparams(("arbitrary",)),
        name="mixer_in",
    )(x, *weights, cos_t, sin_t)


def _attn_kernel(q_ref, k_ref, vt_ref, o_ref, s_ref, mx_ref, m_ref, acc_ref, *, tq, tk):
    qi = pl.program_id(2)
    cw = ATTN_COLS
    ncol = tq // cw
    nkc = tk // LANES
    rw = min(ATTN_ROWS, tk)
    ones_rows = (lax.broadcasted_iota(jnp.int32, (SUM_ROWS, rw), 0) == 0).astype(BF16)

    def score_pass(j, c, masked):
        qc = q_ref[pl.ds(pl.multiple_of(c * cw, cw), cw), :]
        mx = None
        for r in range(tk // rw):
            kb = k_ref[pl.ds(pl.multiple_of(j * tk + r * rw, rw), rw), :]
            st = _mm_nt(kb, qc)
            if masked:
                kpos = r * rw + lax.broadcasted_iota(jnp.int32, (rw, cw), 0)
                qpos = c * cw + lax.broadcasted_iota(jnp.int32, (rw, cw), 1)
                st = jnp.where(kpos <= qpos, st, NEG_BIG)
            s_ref[j % 2, c, r * rw:(r + 1) * rw, :] = st
            mr = jnp.max(st, axis=0, keepdims=True)
            mx = mr if mx is None else jnp.maximum(mx, mr)
        mx_ref[j % 2, c] = mx

    def value_pass(j, c):
        m_old = m_ref[c]
        m_new = jnp.maximum(m_old, mx_ref[j % 2, c])
        pv = None
        for r in range(tk // rw):
            p = jnp.exp2(s_ref[j % 2, c, r * rw:(r + 1) * rw, :] - m_new).astype(BF16)
            vb = jnp.concatenate([vt_ref[j * nkc + r * (rw // LANES) + i] for i in range(rw // LANES)], axis=1)
            vb = jnp.concatenate([vb, ones_rows], axis=0)
            part = _mm(vb, p)
            pv = part if pv is None else pv + part
        acc_ref[c] = jnp.exp2(m_old - m_new) * acc_ref[c] + pv
        m_ref[c] = m_new

    def over_cols(fn):
        def body(c, carry):
            fn(c)
            return carry
        lax.fori_loop(0, ncol, body, 0)

    m_ref[...] = jnp.full(m_ref.shape, NEG_BIG, F32)
    acc_ref[...] = jnp.zeros(acc_ref.shape, F32)

    @pl.when(qi == 0)
    def _():
        over_cols(lambda c: score_pass(0, c, True))

    @pl.when(qi > 0)
    def _():
        over_cols(lambda c: score_pass(0, c, False))

        def both(j):
            def fn(c):
                value_pass(j, c)
                score_pass(j + 1, c, False)
            return fn

        def body(j, carry):
            over_cols(both(j))
            return carry

        lax.fori_loop(0, qi - 1, body, 0)

        def last(c):
            value_pass(qi - 1, c)
            score_pass(qi, c, True)

        over_cols(last)

    over_cols(lambda c: value_pass(qi, c))
    for c in range(ncol):
        acc = acc_ref[c]
        out = acc[:V_HEAD] / acc[V_HEAD:V_HEAD + 1]
        for i in range(cw // LANES):
            o_ref[c * (cw // LANES) + i] = out[:, i * LANES:(i + 1) * LANES].astype(BF16)


def _attention(q, k, vt, batch, seq):
    tq = tk = min(ATTN_TILE, seq)
    nq = seq // tq
    ncol = tq // ATTN_COLS
    kern = functools.partial(_attn_kernel, tq=tq, tk=tk)
    return pl.pallas_call(
        kern,
        grid=(batch, N_HEADS, nq),
        in_specs=[pl.BlockSpec((tq, HEAD_PAD), lambda b, h, i: (b * nq + i, h)),
                  pl.BlockSpec((seq, HEAD_PAD), lambda b, h, i: (b, h)),
                  pl.BlockSpec((seq // LANES, V_HEAD, LANES), lambda b, h, i: (b, h, 0))],
        out_specs=pl.BlockSpec((tq // LANES, V_HEAD, LANES), lambda b, h, i: (b * nq + i, h, 0)),
        out_shape=jax.ShapeDtypeStruct((batch * seq // LANES, N_HEADS * V_HEAD, LANES), BF16),
        scratch_shapes=[pltpu.VMEM((2, ncol, tk, ATTN_COLS), F32),
                        pltpu.VMEM((2, ncol, 1, ATTN_COLS), F32),
                        pltpu.VMEM((ncol, 1, ATTN_COLS), F32),
                        pltpu.VMEM((ncol, V_HEAD + SUM_ROWS, ATTN_COLS), F32)],
        compiler_params=_cparams(("arbitrary", "arbitrary", "arbitrary")),
        name="attention",
    )(q, k, vt)


def _ssm_tables_kernel(lrr_ref, lir_ref, lrc_ref, lic_ref, ldt_ref, btr_ref, bti_ref,
                       cr_ref, ci_ref, ctr_ref, cti_ref,
                       klag_ref, win_ref, wout_ref, apr_ref, api_ref):
    ch = SSM_CHUNK
    dt = jnp.exp(ldt_ref[0])

    def power(lr, li, n):
        mag = jnp.exp(lr * dt * n)
        return mag * jnp.cos(li * dt * n), mag * jnp.sin(li * dt * n)

    lr = jnp.minimum(lrr_ref[0], -1e-4)
    li = lir_ref[0]
    a_re, a_im = power(lr, li, 1.0)
    den = lr * lr + li * li
    nr, ni = a_re - 1.0, a_im
    coef_re = (nr * lr + ni * li) / den
    coef_im = (ni * lr - nr * li) / den
    btr, bti = btr_ref[0], bti_ref[0]
    bbr = coef_re * btr - coef_im * bti
    bbi = coef_re * bti + coef_im * btr
    cr, ci = cr_ref[0], ci_ref[0]
    f_re = jnp.concatenate([cr * bbr[i:i + 1] - ci * bbi[i:i + 1] for i in range(SSM_GROUP)], axis=0)
    f_im = jnp.concatenate([cr * bbi[i:i + 1] + ci * bbr[i:i + 1] for i in range(SSM_GROUP)], axis=0)

    lrc = jnp.minimum(lrc_ref[0], -1e-4)
    lic = lic_ref[0]
    lag = lax.broadcasted_iota(jnp.int32, (1, ch), 1).astype(F32)
    pl_re, pl_im = power(lrc, lic, lag)
    hi = lax.Precision.HIGHEST
    klag_ref[0] = (jnp.dot(f_re, pl_re, precision=hi, preferred_element_type=F32)
                   - jnp.dot(f_im, pl_im, precision=hi, preferred_element_type=F32))

    back = (ch - 1) - lax.broadcasted_iota(jnp.int32, (ch, 1), 0).astype(F32)
    q_re, q_im = power(lr, li, back)
    for i in range(SSM_GROUP):
        w_re = q_re * bbr[i:i + 1] - q_im * bbi[i:i + 1]
        w_im = q_re * bbi[i:i + 1] + q_im * bbr[i:i + 1]
        win_ref[0, i * ch:(i + 1) * ch, :] = jnp.concatenate([w_re, w_im], axis=1).astype(BF16)

    pw_re, pw_im = power(lrc, lic, lag + 1.0)
    ctr, cti = ctr_ref[0], cti_ref[0]
    p = SSM_STATE
    for o in range(SSM_GROUP):
        c_re, c_im = ctr[:, o:o + 1], cti[:, o:o + 1]
        wout_ref[0, :p, o * ch:(o + 1) * ch] = (c_re * pw_re - c_im * pw_im).astype(BF16)
        wout_ref[0, p:, o * ch:(o + 1) * ch] = (-(c_re * pw_im + c_im * pw_re)).astype(BF16)

    apr_ref[0], api_ref[0] = power(lr, li, float(ch))


def _ssm_tables(lam_re, lam_im, log_dt, b_re, b_im, c_re, c_im):
    n = lam_re.shape[0]
    p, g, ch = SSM_STATE, SSM_GROUP, SSM_CHUNK
    args = [lam_re.reshape(n, 1, p), lam_im.reshape(n, 1, p),
            lam_re.reshape(n, p, 1), lam_im.reshape(n, p, 1), log_dt.reshape(n, 1, 1),
            jnp.swapaxes(b_re, 1, 2), jnp.swapaxes(b_im, 1, 2), c_re, c_im,
            jnp.swapaxes(c_re, 1, 2), jnp.swapaxes(c_im, 1, 2)]
    spec = lambda a: pl.BlockSpec((1,) + a.shape[1:], lambda i: (i, 0, 0))
    out_shape = [jax.ShapeDtypeStruct((n, g * g, ch), F32),
                 jax.ShapeDtypeStruct((n, g * ch, 2 * p), BF16),
                 jax.ShapeDtypeStruct((n, 2 * p, g * ch), BF16),
                 jax.ShapeDtypeStruct((n, 1, p), F32),
                 jax.ShapeDtypeStruct((n, 1, p), F32)]
    return pl.pallas_call(
        _ssm_tables_kernel,
        grid=(n,),
        in_specs=[spec(a) for a in args],
        out_specs=[spec(s) for s in out_shape],
        out_shape=out_shape,
        compiler_params=_cparams(("arbitrary",)),
        name="ssm_tables",
    )(*args)


def _ssm_kernel(u_ref, klag_ref, win_ref, wout_ref, apr_ref, api_ref, d_ref, s_ref,
                mt_ref, hin_re, hin_im, hst_re, hst_im, *, batch):
    ch, g, p = SSM_CHUNK, SSM_GROUP, SSM_STATE
    nc = u_ref.shape[0]
    ncb = nc // batch
    causal = (lax.broadcasted_iota(jnp.int32, (ch, ch), 1)
              >= lax.broadcasted_iota(jnp.int32, (ch, ch), 0))

    def build(i, _):
        for o in range(g):
            kv = klag_ref[0, pl.ds(i * g + o, 1), :]
            tz = pltpu.roll(jnp.broadcast_to(kv, (ch, ch)), 0, 1, stride=1, stride_axis=0)
            mt_ref[pl.ds(pl.multiple_of(i * ch, ch), ch), o * ch:(o + 1) * ch] = (
                jnp.where(causal, tz, 0.0).astype(BF16))
        return 0

    lax.fori_loop(0, g, build, 0)

    lhs = jnp.concatenate([u_ref[:, i, :].astype(BF16) for i in range(g)], axis=1)
    y = _mm(lhs, mt_ref[...])
    hin = _mm(lhs, win_ref[0])
    hin_re[...] = hin[:, :p]
    hin_im[...] = hin[:, p:]
    a_re, a_im = apr_ref[0], api_ref[0]

    def scan(c, carry):
        new = []
        for b in range(batch):
            h_re, h_im = carry[b]
            r = b * ncb + c
            hst_re[pl.ds(r, 1), :] = h_re
            hst_im[pl.ds(r, 1), :] = h_im
            x_re = hin_re[pl.ds(r, 1), :]
            x_im = hin_im[pl.ds(r, 1), :]
            new.append((a_re * h_re - a_im * h_im + x_re, a_re * h_im + a_im * h_re + x_im))
        return tuple(new)

    zero = jnp.zeros((1, p), F32)
    lax.fori_loop(0, ncb, scan, tuple((zero, zero) for _ in range(batch)))
    y = y + _mm(hst_re[...].astype(BF16), wout_ref[0, :p, :]) + _mm(hst_im[...].astype(BF16), wout_ref[0, p:, :])
    for o in range(g):
        yo = y[:, o * ch:(o + 1) * ch] + u_ref[:, o, :] * d_ref[0, o:o + 1, :]
        s_ref[:, o, :] = jax.nn.gelu(yo)


def _ssm(ut, klag, win, wout, apr, api, d_skip, layer, batch):
    nc = ut.shape[0]
    g, ch, p = SSM_GROUP, SSM_CHUNK, SSM_STATE
    base = layer * SSM_GROUPS
    tab = lambda a: pl.BlockSpec((1,) + a.shape[1:], lambda i: (base + i, 0, 0))
    kern = functools.partial(_ssm_kernel, batch=batch)
    return pl.pallas_call(
        kern,
        grid=(SSM_GROUPS,),
        in_specs=[pl.BlockSpec((nc, g, LANES), lambda i: (0, i, 0)),
                  tab(klag), tab(win), tab(wout), tab(apr), tab(api),
                  pl.BlockSpec((1, g, 1), lambda i: (i, 0, 0))],
        out_specs=pl.BlockSpec((nc, g, LANES), lambda i: (0, i, 0)),
        out_shape=jax.ShapeDtypeStruct(ut.shape, F32),
        scratch_shapes=[pltpu.VMEM((g * ch, g * ch), BF16),
                        pltpu.VMEM((nc, p), F32), pltpu.VMEM((nc, p), F32),
                        pltpu.VMEM((nc, p), F32), pltpu.VMEM((nc, p), F32)],
        compiler_params=_cparams(("arbitrary",)),
        name="ssm",
    )(ut, klag, win, wout, apr, api, d_skip.reshape(SSM_GROUPS, g, 1))


def _merge_kernel(yt_ref, st_ref, zc_ref, halo_ref, g_ref, x_ref,
                  woa_ref, wglut_ref, bglu_ref, wos_ref, cw_ref, cb_ref, lng_ref, lnb_ref,
                  woc_ref, wout_ref, nf_ref, *rest, tiles_per_seq, with_router):
    if with_router:
        wr_ref, xn_ref, h2_ref, route_ref, zs_ref = rest
    else:
        xn_ref, h2_ref, zs_ref = rest
    tm = x_ref.shape[0]
    nch = tm // LANES
    i = pl.program_id(0)

    yt = jnp.concatenate([yt_ref[c] for c in range(nch)], axis=1)
    y_attn = _mm_tn(yt, woa_ref[...])

    st = jnp.concatenate([st_ref[c] for c in range(nch)], axis=1)
    gate = jax.nn.sigmoid(_mm(wglut_ref[...], st.astype(BF16)) + bglu_ref[...])
    y_ssm = _mm_tn((st * gate).astype(BF16), wos_ref[...])

    first = (i % tiles_per_seq) == 0
    zs_ref[:CONV_HALO, :] = jnp.where(first, 0.0, halo_ref[...])
    zs_ref[CONV_HALO:, :] = zc_ref[...]
    off = CONV_HALO - (CONV_K - 1)
    conv = jnp.zeros((tm, CONV_WIDTH), F32)
    for kk in range(CONV_K):
        conv = conv + zs_ref[pl.ds(off + kk, tm), :] * cw_ref[kk:kk + 1, :]
    conv = conv + cb_ref[...]
    mu = jnp.mean(conv, axis=-1, keepdims=True)
    cen = conv - mu
    var = jnp.mean(cen * cen, axis=-1, keepdims=True)
    yc = jax.nn.silu(cen * lax.rsqrt(var + LN_EPS) * lng_ref[...] + lnb_ref[...])
    y_conv = _mm(yc.astype(BF16), woc_ref[...])

    d = D_MODEL
    merged = (g_ref[:, :d].astype(F32) * y_attn + g_ref[:, d:2 * d].astype(F32) * y_ssm
              + g_ref[:, 2 * d:].astype(F32) * y_conv)
    xn = x_ref[...] + _mm(merged.astype(BF16), wout_ref[...])
    xn_ref[...] = xn
    h = _rms(xn, nf_ref[...])
    h2_ref[...] = h.astype(h2_ref.dtype)

    if with_router:
        logits = jnp.dot(h, wr_ref[...], precision=lax.Precision.HIGHEST, preferred_element_type=F32)
        lane = lax.broadcasted_iota(jnp.int32, (tm, LANES), 1)
        logits = jnp.where(lane < N_EXPERTS, logits, NEG_BIG)
        m1 = jnp.max(logits, axis=-1, keepdims=True)
        i1 = jnp.min(jnp.where(logits == m1, lane, LANES), axis=-1, keepdims=True)
        rest_l = jnp.where(lane == i1, NEG_BIG, logits)
        m2 = jnp.max(rest_l, axis=-1, keepdims=True)
        i2 = jnp.min(jnp.where(rest_l == m2, lane, LANES), axis=-1, keepdims=True)
        e = jnp.exp(m2 - m1)
        g1 = 1.0 / (1.0 + e)
        g2 = e / (1.0 + e)
        route_ref[...] = jnp.where(lane == 0, i1.astype(F32),
                         jnp.where(lane == 1, i2.astype(F32),
                         jnp.where(lane == 2, g1, jnp.where(lane == 3, g2, 0.0))))


def _merge(yt, st, zc, gates, x, w, seq, with_router):
    t = x.shape[0]
    tm = min(TOKEN_TILE, t, seq)
    nch = tm // LANES
    hpt = tm // CONV_HALO
    row = lambda c: pl.BlockSpec((tm, c), lambda i: (i, 0))
    chunked = lambda c: pl.BlockSpec((nch, c, LANES), lambda i: (i, 0, 0))
    weights = [w["woa"], w["wglut"], w["bglu"], w["wos"], w["cw"], w["cb"], w["lng"], w["lnb"],
               w["woc"], w["wout"], w["nf"]]
    out_shape = [jax.ShapeDtypeStruct((t, D_MODEL), F32),
                 jax.ShapeDtypeStruct((t, D_MODEL), F32 if with_router else BF16)]
    out_specs = [row(D_MODEL), row(D_MODEL)]
    if with_router:
        weights.append(w["wr"])
        out_shape.append(jax.ShapeDtypeStruct((t, LANES), F32))
        out_specs.append(row(LANES))
    kern = functools.partial(_merge_kernel, tiles_per_seq=seq // tm, with_router=with_router)
    return pl.pallas_call(
        kern,
        grid=(t // tm,),
        in_specs=[chunked(N_HEADS * V_HEAD), chunked(SSM_WIDTH), row(CONV_WIDTH),
                  pl.BlockSpec((CONV_HALO, CONV_WIDTH), lambda i: (jnp.maximum(i * hpt - 1, 0), 0)),
                  row(3 * D_MODEL), row(D_MODEL)] + [_const_spec(a.shape) for a in weights],
        out_specs=out_specs,
        out_shape=out_shape,
        scratch_shapes=[pltpu.VMEM((CONV_HALO + tm, CONV_WIDTH), F32)],
        compiler_params=_cparams(("arbitrary",)),
        name="merge_router" if with_router else "merge",
    )(yt, st, zc, zc, gates, x, *weights)


def _ffn_kernel(x_ref, h_ref, w1_ref, w3_ref, w2_ref, o_ref):
    h = h_ref[...]
    f = w1_ref.shape[1]
    acc = x_ref[...]
    for lo in range(0, f, FFN_F_TILE):
        hi = min(lo + FFN_F_TILE, f)
        a = _mm(h, w1_ref[:, lo:hi])
        b = _mm(h, w3_ref[:, lo:hi])
        acc = acc + _mm((jax.nn.silu(a) * b).astype(BF16), w2_ref[lo:hi, :])
    o_ref[...] = acc


def _ffn(x, h, w1, w3, w2):
    t = x.shape[0]
    tm = min(TOKEN_TILE, t)
    row = pl.BlockSpec((tm, D_MODEL), lambda i: (i, 0))
    return pl.pallas_call(
        _ffn_kernel,
        grid=(t // tm,),
        in_specs=[row, row, _const_spec(w1.shape), _const_spec(w3.shape), _const_spec(w2.shape)],
        out_specs=row,
        out_shape=jax.ShapeDtypeStruct((t, D_MODEL), F32),
        compiler_params=_cparams(("arbitrary",)),
        name="ffn",
    )(x, h, w1, w3, w2)


def _dispatch_kernel(dest_ref, h_ref, xs_in_ref, xs_ref, sem):
    del xs_in_ref
    n = dest_ref.shape[2]

    def row_copy(r):
        return pltpu.make_async_copy(h_ref.at[pl.ds(r // TOP_K, 1), :],
                                     xs_ref.at[pl.ds(dest_ref[0, 0, r], 1), :], sem)

    def issue(r, c):
        row_copy(r).start()
        return c

    def drain(r, c):
        row_copy(r).wait()
        return c

    lax.fori_loop(0, n, issue, 0)
    lax.fori_loop(0, n, drain, 0)


def _dispatch(h, dest, rows):
    t = h.shape[0]
    tm = min(ROUTE_TILE, t)
    nt = t // tm
    xs0 = jnp.zeros((rows, D_MODEL), h.dtype)
    return pl.pallas_call(
        _dispatch_kernel,
        grid=(nt,),
        in_specs=[pl.BlockSpec((1, 1, TOP_K * tm), lambda i: (i, 0, 0), memory_space=pltpu.SMEM),
                  pl.BlockSpec((tm, D_MODEL), lambda i: (i, 0)),
                  pl.BlockSpec(memory_space=pl.ANY)],
        out_specs=pl.BlockSpec(memory_space=pl.ANY),
        out_shape=jax.ShapeDtypeStruct((rows, D_MODEL), h.dtype),
        scratch_shapes=[pltpu.SemaphoreType.DMA(())],
        input_output_aliases={2: 0},
        compiler_params=_cparams(("arbitrary",)),
        name="moe_dispatch",
    )(dest.reshape(nt, 1, TOP_K * tm), h, xs0)


def _experts_kernel(te_ref, nu_ref, x_ref, w1_ref, w3_ref, w2_ref, y_ref, xb_ref):
    i = pl.program_id(0)
    j = pl.program_id(1)
    used = i < nu_ref[0]

    @pl.when(jnp.logical_and(used, j == 0))
    def _():
        xb_ref[...] = x_ref[...].astype(BF16)

    @pl.when(jnp.logical_and(jnp.logical_not(used), j == 0))
    def _():
        y_ref[...] = jnp.zeros_like(y_ref)

    @pl.when(used)
    def _():
        xb = xb_ref[...]
        a = _mm(xb, w1_ref[0])
        b = _mm(xb, w3_ref[0])
        part = _mm((jax.nn.silu(a) * b).astype(BF16), w2_ref[0])

        @pl.when(j == 0)
        def _():
            y_ref[...] = part

        @pl.when(j > 0)
        def _():
            y_ref[...] += part


def _experts(xs, tile_expert, n_used, w1, w3, w2):
    rows = xs.shape[0]
    tm = MOE_ROW_TILE
    f = w1.shape[2]
    tf = min(MOE_F_TILE, f)
    nf = f // tf

    def fchunk(i, j, nu):
        return jnp.where(i < nu[0], j, nf - 1)

    grid_spec = pltpu.PrefetchScalarGridSpec(
        num_scalar_prefetch=2,
        grid=(rows // tm, nf),
        in_specs=[pl.BlockSpec((tm, D_MODEL), lambda i, j, te, nu: (i, 0)),
                  pl.BlockSpec((1, D_MODEL, tf), lambda i, j, te, nu: (te[i], 0, fchunk(i, j, nu))),
                  pl.BlockSpec((1, D_MODEL, tf), lambda i, j, te, nu: (te[i], 0, fchunk(i, j, nu))),
                  pl.BlockSpec((1, tf, D_MODEL), lambda i, j, te, nu: (te[i], fchunk(i, j, nu), 0))],
        out_specs=pl.BlockSpec((tm, D_MODEL), lambda i, j, te, nu: (i, 0)),
        scratch_shapes=[pltpu.VMEM((tm, D_MODEL), BF16)],
    )
    return pl.pallas_call(
        _experts_kernel,
        grid_spec=grid_spec,
        out_shape=jax.ShapeDtypeStruct((rows, D_MODEL), F32),
        compiler_params=_cparams(("arbitrary", "arbitrary")),
        name="moe_experts",
    )(tile_expert, n_used, xs, w1, w3, w2)


def _combine_kernel(dest_ref, x_ref, route_ref, nfin_ref, ys_ref, o_ref, buf_ref, sem):
    tm = x_ref.shape[0]
    n = TOP_K * tm

    def row_copy(r):
        slot = (r % TOP_K) * tm + r // TOP_K
        return pltpu.make_async_copy(ys_ref.at[pl.ds(dest_ref[0, 0, r], 1), :],
                                     buf_ref.at[pl.ds(slot, 1), :], sem)

    def issue(r, c):
        row_copy(r).start()
        return c

    def drain(r, c):
        row_copy(r).wait()
        return c

    lax.fori_loop(0, n, issue, 0)
    lax.fori_loop(0, n, drain, 0)
    route = route_ref[...]
    y = x_ref[...] + route[:, 2:3] * buf_ref[:tm, :] + route[:, 3:4] * buf_ref[tm:, :]
    o_ref[...] = _rms(y, nfin_ref[...])


def _combine(x, route, dest, ys, norm_final):
    t = x.shape[0]
    tm = min(ROUTE_TILE, t)
    nt = t // tm
    return pl.pallas_call(
        _combine_kernel,
        grid=(nt,),
        in_specs=[pl.BlockSpec((1, 1, TOP_K * tm), lambda i: (i, 0, 0), memory_space=pltpu.SMEM),
                  pl.BlockSpec((tm, D_MODEL), lambda i: (i, 0)),
                  pl.BlockSpec((tm, LANES), lambda i: (i, 0)),
                  _const_spec((1, D_MODEL)),
                  pl.BlockSpec(memory_space=pl.ANY)],
        out_specs=pl.BlockSpec((tm, D_MODEL), lambda i: (i, 0)),
        out_shape=jax.ShapeDtypeStruct((t, D_MODEL), F32),
        scratch_shapes=[pltpu.VMEM((TOP_K * tm, D_MODEL), F32), pltpu.SemaphoreType.DMA(())],
        compiler_params=_cparams(("arbitrary",)),
        name="moe_combine",
    )(dest.reshape(nt, 1, TOP_K * tm), x, route, norm_final.reshape(1, D_MODEL), ys)


def _moe_plan(route, row_tile):
    t = route.shape[0]
    m = t * TOP_K
    e_flat = route[:, :TOP_K].astype(jnp.int32).reshape(m)
    onehot = (e_flat[:, None] == jnp.arange(N_EXPERTS, dtype=jnp.int32)[None, :]).astype(jnp.int32)
    csum = jnp.cumsum(onehot, axis=0)
    rank = jnp.sum((csum - onehot) * onehot, axis=1)
    counts = csum[-1]
    padded = ((counts + row_tile - 1) // row_tile) * row_tile
    pad_ends = jnp.cumsum(padded)
    pad_starts = pad_ends - padded
    dest = jnp.sum(onehot * pad_starts[None, :], axis=1) + rank
    n_tiles = -(-m // row_tile) + N_EXPERTS
    tile_start = jnp.arange(n_tiles, dtype=jnp.int32) * row_tile
    tile_expert = jnp.minimum(jnp.sum((pad_ends[None, :] <= tile_start[:, None]).astype(jnp.int32), axis=1),
                              N_EXPERTS - 1)
    n_used = (pad_ends[-1] // row_tile).reshape(1)
    return dest.astype(jnp.int32), tile_expert.astype(jnp.int32), n_used.astype(jnp.int32), n_tiles * row_tile


def _prep_mixer_weights(l, w_in, q_norm, w_uq, kv_norm, w_ukv, w_o_attn, ssm_w_glu, ssm_b_glu, w_o_ssm,
                        conv_w, conv_b, conv_ln_g, conv_ln_b, w_o_conv, w_out, norm_ffn):
    o = IN_OFFSETS
    wi = w_in[l]
    low = wi[:, o[0]:o[3]]
    wa = jnp.pad(low, ((0, 0), (0, 4 * LANES - low.shape[1]))).astype(BF16)
    wq = jnp.pad(w_uq[l].reshape(Q_LORA, N_HEADS, QK_HEAD),
                 ((0, 0), (0, 0), (0, HEAD_PAD - QK_HEAD))).reshape(Q_LORA, N_HEADS * HEAD_PAD).astype(BF16)
    wkv = w_ukv[l].reshape(KV_LORA, N_HEADS, QK_NOPE + V_HEAD)
    wkn = jnp.pad(wkv[..., :QK_NOPE], ((0, 0), (0, 0), (0, HEAD_PAD - QK_NOPE))
                  ).reshape(KV_LORA, N_HEADS * HEAD_PAD).astype(BF16)
    wvt = wkv[..., QK_NOPE:].reshape(KV_LORA, N_HEADS * V_HEAD).T.astype(BF16)
    place = np.zeros((LANES, N_HEADS * HEAD_PAD), np.float32)
    for h in range(N_HEADS):
        for r in range(QK_ROPE):
            place[r, h * HEAD_PAD + QK_NOPE + r] = 1.0
    return dict(
        wa=wa, wut=wi[:, o[3]:o[4]].T.astype(BF16), wca=wi[:, o[4]:o[5]].astype(BF16),
        wcg=wi[:, o[5]:o[6]].astype(BF16), wg=wi[:, o[6]:o[7]].astype(BF16),
        qn=q_norm[l].reshape(1, Q_LORA), kvn=kv_norm[l].reshape(1, KV_LORA),
        wq=wq, wkn=wkn, wke=jnp.asarray(place, BF16), wvt=wvt,
        woa=w_o_attn[l].astype(BF16), wglut=ssm_w_glu[l].T.astype(BF16),
        bglu=ssm_b_glu[l].reshape(SSM_WIDTH, 1), wos=w_o_ssm[l].astype(BF16),
        cw=conv_w[l], cb=conv_b[l].reshape(1, CONV_WIDTH), lng=conv_ln_g[l].reshape(1, CONV_WIDTH),
        lnb=conv_ln_b[l].reshape(1, CONV_WIDTH), woc=w_o_conv[l].astype(BF16),
        wout=w_out[l].astype(BF16), nf=norm_ffn[l].reshape(1, D_MODEL))


def kernel(x, positions, norm_mix, w_in, q_norm, w_uq, kv_norm, w_ukv, w_o_attn, ssm_lam_re, ssm_lam_im, ssm_log_dt, ssm_b_re, ssm_b_im, ssm_c_re, ssm_c_im, ssm_d, ssm_w_glu, ssm_b_glu, w_o_ssm, conv_w, conv_b, conv_ln_g, conv_ln_b, w_o_conv, w_out, norm_ffn, ffn_w1, ffn_w3, ffn_w2, moe_router, moe_w1, moe_w3, moe_w2, norm_final):
    batch, seq, d = x.shape
    depth = w_in.shape[0]
    assert d == D_MODEL and depth == 2 and seq % SSM_CHUNK == 0
    t = batch * seq
    xf = x.reshape(t, d)
    cos_t, sin_t = _rope_tables(positions)
    ng = depth * SSM_GROUPS
    klag, win, wout_s, apr, api = _ssm_tables(
        ssm_lam_re.reshape(ng, SSM_STATE), ssm_lam_im.reshape(ng, SSM_STATE), ssm_log_dt.reshape(ng),
        ssm_b_re.reshape(ng, SSM_STATE, SSM_GROUP), ssm_b_im.reshape(ng, SSM_STATE, SSM_GROUP),
        ssm_c_re.reshape(ng, SSM_GROUP, SSM_STATE), ssm_c_im.reshape(ng, SSM_GROUP, SSM_STATE))

    for layer in range(depth):
        moe_layer = layer % 2 == 1
        w = _prep_mixer_weights(layer, w_in, q_norm, w_uq, kv_norm, w_ukv, w_o_attn, ssm_w_glu, ssm_b_glu,
                                w_o_ssm, conv_w, conv_b, conv_ln_g, conv_ln_b, w_o_conv, w_out, norm_ffn)
        q, k, vt, ut, zc, gates = _mixer_in(xf, norm_mix[layer].reshape(1, d), w, cos_t, sin_t)
        yt = _attention(q, k, vt, batch, seq)
        st = _ssm(ut, klag, win, wout_s, apr, api, ssm_d[layer], layer, batch)
        i = layer // 2
        if not moe_layer:
            xn, h2 = _merge(yt, st, zc, gates, xf, w, seq, with_router=False)
            xf = _ffn(xn, h2, ffn_w1[i].astype(BF16), ffn_w3[i].astype(BF16), ffn_w2[i].astype(BF16))
        else:
            w["wr"] = jnp.pad(moe_router[i], ((0, 0), (0, LANES - N_EXPERTS)))
            xn, h2, route = _merge(yt, st, zc, gates, xf, w, seq, with_router=True)
            dest, tile_expert, n_used, rows = _moe_plan(route, MOE_ROW_TILE)
            xs = _dispatch(h2, dest, rows)
            ys = _experts(xs, tile_expert, n_used, moe_w1[i].astype(BF16), moe_w3[i].astype(BF16),
                          moe_w2[i].astype(BF16))
            xf = _combine(xn, route, dest, ys, norm_final)
    return xf.reshape(batch, seq, d)
```

```python
import functools

import numpy as np
import jax
import jax.numpy as jnp
from jax import lax
from jax.experimental import pallas as pl
from jax.experimental.pallas import tpu as pltpu

F32 = jnp.float32
BF16 = jnp.bfloat16

D_MODEL = 1024
N_HEADS = 8
Q_LORA = 256
KV_LORA = 128
QK_NOPE = 64
QK_ROPE = 32
QK_HEAD = QK_NOPE + QK_ROPE
V_HEAD = 64
ROPE_THETA = 10000.0
SSM_WIDTH = 512
SSM_GROUP = 16
SSM_GROUPS = SSM_WIDTH // SSM_GROUP
SSM_STATE = 64
SSM_CHUNK = 128
CONV_WIDTH = 512
CONV_K = 31
N_EXPERTS = 8
TOP_K = 2
RMS_EPS = 1e-6
LN_EPS = 1e-5
IN_SPLITS = (Q_LORA, KV_LORA, QK_ROPE, SSM_WIDTH, CONV_WIDTH, CONV_WIDTH, 3 * D_MODEL)
IN_OFFSETS = tuple(int(v) for v in np.cumsum((0,) + IN_SPLITS))

LANES = 128
SUBLANES = 8
HEAD_PAD = LANES
VMEM_LIMIT_BYTES = 56 * 1024 * 1024

TOKEN_TILE = 512
ATTN_TILE = 1024
ATTN_COLS = 256
ATTN_ROWS = 256
ATTN_UNROLL = 4
MOE_ROW_TILE = 1024
MOE_F_TILE = 512
FFN_F_TILE = 512
ROUTE_TILE = 256
ROW_DMA_UNROLL = 8
CONV_HALO = 32
SUM_ROWS = 16
NEG_BIG = -1e30
LOG2_E = float(np.log2(np.e))


def _cparams(sem):
    return pltpu.CompilerParams(dimension_semantics=sem, vmem_limit_bytes=VMEM_LIMIT_BYTES)


def _const_spec(shape):
    nd = len(shape)
    return pl.BlockSpec(shape, lambda *_: (0,) * nd, pipeline_mode=pl.Buffered(1))


def _rms(x, g):
    return x * lax.rsqrt(jnp.mean(x * x, axis=-1, keepdims=True) + RMS_EPS) * g


def _mm(a, b):
    return jnp.dot(a, b, preferred_element_type=F32)


def _mm_nt(a, b):
    return lax.dot_general(a, b, (((1,), (1,)), ((), ())), preferred_element_type=F32)


def _mm_tn(a, b):
    return lax.dot_general(a, b, (((0,), (0,)), ((), ())), preferred_element_type=F32)


def _rope_tables_kernel(pos_ref, invf_ref, sign_ref, cos_ref, sin_ref):
    ang = pos_ref[...].astype(F32) * invf_ref[...]
    cos_ref[...] = jnp.cos(ang)
    sin_ref[...] = jnp.sin(ang) * sign_ref[...]


def _rope_tables(positions):
    t = positions.size
    tm = min(2048, t)
    half = QK_ROPE // 2
    inv_freq = ROPE_THETA ** (-np.arange(0, QK_ROPE, 2, dtype=np.float32) / QK_ROPE)
    invf = np.zeros((1, LANES), np.float32)
    sign = np.zeros((1, LANES), np.float32)
    invf[0, QK_NOPE:QK_NOPE + half] = inv_freq
    invf[0, QK_NOPE + half:QK_HEAD] = inv_freq
    sign[0, QK_NOPE:QK_NOPE + half] = -1.0
    sign[0, QK_NOPE + half:QK_HEAD] = 1.0
    return pl.pallas_call(
        _rope_tables_kernel,
        grid=(t // tm,),
        in_specs=[pl.BlockSpec((tm, 1), lambda i: (i, 0)),
                  pl.BlockSpec((1, LANES), lambda i: (0, 0)),
                  pl.BlockSpec((1, LANES), lambda i: (0, 0))],
        out_specs=[pl.BlockSpec((tm, LANES), lambda i: (i, 0))] * 2,
        out_shape=[jax.ShapeDtypeStruct((t, LANES), F32)] * 2,
        compiler_params=_cparams(("arbitrary",)),
        name="rope_tables",
    )(positions.reshape(t, 1), jnp.asarray(invf), jnp.asarray(sign))


def _rope_head(xh, cos, sin, lane):
    half = QK_ROPE // 2
    swapped = jnp.where(lane < QK_NOPE + half,
                        pltpu.roll(xh, LANES - half, 1), pltpu.roll(xh, half, 1))
    return xh * cos + swapped * sin


def _mixer_in_kernel(x_ref, nm_ref, wa_ref, wut_ref, wca_ref, wcg_ref, wg_ref, qn_ref, kvn_ref,
                     wq_ref, wkn_ref, wke_ref, wvt_ref, cos_ref, sin_ref,
                     q_ref, k_ref, vt_ref, ut_ref, zc_ref, g_ref):
    tm = x_ref.shape[0]
    hn = _rms(x_ref[...], nm_ref[...]).astype(BF16)
    za = _mm(hn, wa_ref[...])
    cqn = _rms(za[:, :Q_LORA], qn_ref[...]).astype(BF16)
    ckvn = _rms(za[:, Q_LORA:Q_LORA + KV_LORA], kvn_ref[...]).astype(BF16)
    kpe = za[:, Q_LORA + KV_LORA:].astype(BF16)
    q = _mm(cqn, wq_ref[...])
    k = _mm(ckvn, wkn_ref[...]) + _mm(kpe, wke_ref[...])
    cos = cos_ref[...]
    sin = sin_ref[...]
    lane = lax.broadcasted_iota(jnp.int32, (tm, LANES), 1)
    scale = QK_HEAD ** -0.5 * LOG2_E
    for h in range(N_HEADS):
        sl = slice(h * HEAD_PAD, (h + 1) * HEAD_PAD)
        q_ref[:, sl] = (_rope_head(q[:, sl], cos, sin, lane) * scale).astype(BF16)
        k_ref[:, sl] = _rope_head(k[:, sl], cos, sin, lane).astype(BF16)
    vt = _mm_nt(wvt_ref[...], ckvn)
    ut = _mm_nt(wut_ref[...], hn)
    for c in range(tm // LANES):
        sl = slice(c * LANES, (c + 1) * LANES)
        vt_ref[c] = vt[:, sl].astype(BF16)
        ut_ref[c] = ut[:, sl]
    zc_ref[...] = _mm(hn, wca_ref[...]) * jax.nn.sigmoid(_mm(hn, wcg_ref[...]))
    for c in range(3):
        sl = slice(c * D_MODEL, (c + 1) * D_MODEL)
        g_ref[:, sl] = jax.nn.sigmoid(_mm(hn, wg_ref[:, sl])).astype(BF16)


def _mixer_in(x, nm, w, cos_t, sin_t):
    t = x.shape[0]
    tm = min(TOKEN_TILE, t)
    nch = tm // LANES
    row = lambda c: pl.BlockSpec((tm, c), lambda i: (i, 0))
    chunked = lambda c: pl.BlockSpec((nch, c, LANES), lambda i: (i, 0, 0))
    weights = [nm, w["wa"], w["wut"], w["wca"], w["wcg"], w["wg"], w["qn"], w["kvn"],
               w["wq"], w["wkn"], w["wke"], w["wvt"]]
    return pl.pallas_call(
        _mixer_in_kernel,
        grid=(t // tm,),
        in_specs=[row(D_MODEL)] + [_const_spec(a.shape) for a in weights] + [row(LANES), row(LANES)],
        out_specs=[row(N_HEADS * HEAD_PAD), row(N_HEADS * HEAD_PAD), chunked(N_HEADS * V_HEAD),
                   chunked(SSM_WIDTH), row(CONV_WIDTH), row(3 * D_MODEL)],
        out_shape=[jax.ShapeDtypeStruct((t, N_HEADS * HEAD_PAD), BF16),
                   jax.ShapeDtypeStruct((t, N_HEADS * HEAD_PAD), BF16),
                   jax.ShapeDtypeStruct((t // LANES, N_HEADS * V_HEAD, LANES), BF16),
                   jax.ShapeDtypeStruct((t // LANES, SSM_WIDTH, LANES), F32),
                   jax.ShapeDtypeStruct((t, CONV_WIDTH), F32),
                   jax.ShapeDtypeStruct((t, 3 * D_MODEL), BF16)],
        compiler_params=_cparams(("arbitrary",)),
        name="mixer_in",
    )(x, *weights, cos_t, sin_t)


def _attn_kernel(q_ref, k_ref, vt_ref, o_ref, sa_ref, sb_ref, mxa_ref, mxb_ref, m_ref, acc_ref, *, tq, tk):
    qi = pl.program_id(2)
    cw = ATTN_COLS
    ncol = tq // cw
    nkc = tk // LANES
    rw = min(ATTN_ROWS, tk)
    ones_rows = (lax.broadcasted_iota(jnp.int32, (SUM_ROWS, rw), 0) == 0).astype(BF16)
    ping = (sa_ref, mxa_ref)
    pong = (sb_ref, mxb_ref)

    def score_pass(j, c, dst, masked):
        s_ref, mx_ref = dst
        qc = q_ref[pl.ds(pl.multiple_of(c * cw, cw), cw), :]
        mx = None
        for r in range(tk // rw):
            kb = k_ref[pl.ds(pl.multiple_of(j * tk + r * rw, rw), rw), :]
            st = _mm_nt(kb, qc)
            if masked:
                kpos = r * rw + lax.broadcasted_iota(jnp.int32, (rw, cw), 0)
                qpos = c * cw + lax.broadcasted_iota(jnp.int32, (rw, cw), 1)
                st = jnp.where(kpos <= qpos, st, NEG_BIG)
            s_ref[c, r * rw:(r + 1) * rw, :] = st
            mr = jnp.max(st, axis=0, keepdims=True)
            mx = mr if mx is None else jnp.maximum(mx, mr)
        mx_ref[c] = mx

    def value_pass(j, c, src):
        s_ref, mx_ref = src
        m_old = m_ref[c]
        m_new = jnp.maximum(m_old, mx_ref[c])
        pv = None
        for r in range(tk // rw):
            p = jnp.exp2(s_ref[c, r * rw:(r + 1) * rw, :] - m_new).astype(BF16)
            vb = jnp.concatenate([vt_ref[j * nkc + r * (rw // LANES) + i] for i in range(rw // LANES)], axis=1)
            vb = jnp.concatenate([vb, ones_rows], axis=0)
            part = _mm(vb, p)
            pv = part if pv is None else pv + part
        acc_ref[c] = jnp.exp2(m_old - m_new) * acc_ref[c] + pv
        m_ref[c] = m_new

    def over_cols(fn):
        def body(c, carry):
            fn(c)
            return carry
        lax.fori_loop(0, ncol, body, 0, unroll=ATTN_UNROLL)

    def step(j, src, dst, masked):
        def fn(c):
            value_pass(j, c, src)
            score_pass(j + 1, c, dst, masked)
        over_cols(fn)

    m_ref[...] = jnp.full(m_ref.shape, NEG_BIG, F32)
    acc_ref[...] = jnp.zeros(acc_ref.shape, F32)

    @pl.when(qi == 0)
    def _():
        over_cols(lambda c: score_pass(0, c, ping, True))

    @pl.when(qi > 0)
    def _():
        over_cols(lambda c: score_pass(0, c, ping, False))

    n_plain = jnp.maximum(qi - 1, 0)

    def pair(i, carry):
        step(2 * i, ping, pong, False)
        step(2 * i + 1, pong, ping, False)
        return carry

    lax.fori_loop(0, n_plain // 2, pair, 0)

    @pl.when(n_plain % 2 == 1)
    def _():
        step(n_plain - 1, ping, pong, False)

    @pl.when(jnp.logical_and(qi > 0, qi % 2 == 1))
    def _():
        step(qi - 1, ping, pong, True)
        over_cols(lambda c: value_pass(qi, c, pong))

    @pl.when(jnp.logical_and(qi > 0, qi % 2 == 0))
    def _():
        step(qi - 1, pong, ping, True)

    @pl.when(qi % 2 == 0)
    def _():
        over_cols(lambda c: value_pass(qi, c, ping))

    for c in range(ncol):
        acc = acc_ref[c]
        out = acc[:V_HEAD] / acc[V_HEAD:V_HEAD + 1]
        for i in range(cw // LANES):
            o_ref[c * (cw // LANES) + i] = out[:, i * LANES:(i + 1) * LANES].astype(BF16)


def _attention(q, k, vt, batch, seq):
    tq = tk = min(ATTN_TILE, seq)
    nq = seq // tq
    ncol = tq // ATTN_COLS
    kern = functools.partial(_attn_kernel, tq=tq, tk=tk)
    return pl.pallas_call(
        kern,
        grid=(batch, N_HEADS, nq),
        in_specs=[pl.BlockSpec((tq, HEAD_PAD), lambda b, h, i: (b * nq + i, h)),
                  pl.BlockSpec((seq, HEAD_PAD), lambda b, h, i: (b, h)),
                  pl.BlockSpec((seq // LANES, V_HEAD, LANES), lambda b, h, i: (b, h, 0))],
        out_specs=pl.BlockSpec((tq // LANES, V_HEAD, LANES), lambda b, h, i: (b * nq + i, h, 0)),
        out_shape=jax.ShapeDtypeStruct((batch * seq // LANES, N_HEADS * V_HEAD, LANES), BF16),
        scratch_shapes=[pltpu.VMEM((ncol, tk, ATTN_COLS), F32),
                        pltpu.VMEM((ncol, tk, ATTN_COLS), F32),
                        pltpu.VMEM((ncol, 1, ATTN_COLS), F32),
                        pltpu.VMEM((ncol, 1, ATTN_COLS), F32),
                        pltpu.VMEM((ncol, 1, ATTN_COLS), F32),
                        pltpu.VMEM((ncol, V_HEAD + SUM_ROWS, ATTN_COLS), F32)],
        compiler_params=_cparams(("arbitrary", "arbitrary", "arbitrary")),
        name="attention",
    )(q, k, vt)


def _ssm_tables_kernel(lrr_ref, lir_ref, lrc_ref, lic_ref, ldt_ref, btr_ref, bti_ref,
                       cr_ref, ci_ref, ctr_ref, cti_ref,
                       klag_ref, win_ref, wout_ref, apr_ref, api_ref):
    ch = SSM_CHUNK
    dt = jnp.exp(ldt_ref[0])

    def power(lr, li, n):
        mag = jnp.exp(lr * dt * n)
        return mag * jnp.cos(li * dt * n), mag * jnp.sin(li * dt * n)

    lr = jnp.minimum(lrr_ref[0], -1e-4)
    li = lir_ref[0]
    a_re, a_im = power(lr, li, 1.0)
    den = lr * lr + li * li
    nr, ni = a_re - 1.0, a_im
    coef_re = (nr * lr + ni * li) / den
    coef_im = (ni * lr - nr * li) / den
    btr, bti = btr_ref[0], bti_ref[0]
    bbr = coef_re * btr - coef_im * bti
    bbi = coef_re * bti + coef_im * btr
    cr, ci = cr_ref[0], ci_ref[0]
    f_re = jnp.concatenate([cr * bbr[i:i + 1] - ci * bbi[i:i + 1] for i in range(SSM_GROUP)], axis=0)
    f_im = jnp.concatenate([cr * bbi[i:i + 1] + ci * bbr[i:i + 1] for i in range(SSM_GROUP)], axis=0)

    lrc = jnp.minimum(lrc_ref[0], -1e-4)
    lic = lic_ref[0]
    lag = lax.broadcasted_iota(jnp.int32, (1, ch), 1).astype(F32)
    pl_re, pl_im = power(lrc, lic, lag)
    hi = lax.Precision.HIGHEST
    klag_ref[0] = (jnp.dot(f_re, pl_re, precision=hi, preferred_element_type=F32)
                   - jnp.dot(f_im, pl_im, precision=hi, preferred_element_type=F32))

    back = (ch - 1) - lax.broadcasted_iota(jnp.int32, (ch, 1), 0).astype(F32)
    q_re, q_im = power(lr, li, back)
    for i in range(SSM_GROUP):
        w_re = q_re * bbr[i:i + 1] - q_im * bbi[i:i + 1]
        w_im = q_re * bbi[i:i + 1] + q_im * bbr[i:i + 1]
        win_ref[0, i * ch:(i + 1) * ch, :] = jnp.concatenate([w_re, w_im], axis=1).astype(BF16)

    pw_re, pw_im = power(lrc, lic, lag + 1.0)
    ctr, cti = ctr_ref[0], cti_ref[0]
    p = SSM_STATE
    for o in range(SSM_GROUP):
        c_re, c_im = ctr[:, o:o + 1], cti[:, o:o + 1]
        wout_ref[0, :p, o * ch:(o + 1) * ch] = (c_re * pw_re - c_im * pw_im).astype(BF16)
        wout_ref[0, p:, o * ch:(o + 1) * ch] = (-(c_re * pw_im + c_im * pw_re)).astype(BF16)

    apr_ref[0], api_ref[0] = power(lr, li, float(ch))


def _ssm_tables(lam_re, lam_im, log_dt, b_re, b_im, c_re, c_im):
    n = lam_re.shape[0]
    p, g, ch = SSM_STATE, SSM_GROUP, SSM_CHUNK
    args = [lam_re.reshape(n, 1, p), lam_im.reshape(n, 1, p),
            lam_re.reshape(n, p, 1), lam_im.reshape(n, p, 1), log_dt.reshape(n, 1, 1),
            jnp.swapaxes(b_re, 1, 2), jnp.swapaxes(b_im, 1, 2), c_re, c_im,
            jnp.swapaxes(c_re, 1, 2), jnp.swapaxes(c_im, 1, 2)]
    spec = lambda a: pl.BlockSpec((1,) + a.shape[1:], lambda i: (i, 0, 0))
    out_shape = [jax.ShapeDtypeStruct((n, g * g, ch), F32),
                 jax.ShapeDtypeStruct((n, g * ch, 2 * p), BF16),
                 jax.ShapeDtypeStruct((n, 2 * p, g * ch), BF16),
                 jax.ShapeDtypeStruct((n, 1, p), F32),
                 jax.ShapeDtypeStruct((n, 1, p), F32)]
    return pl.pallas_call(
        _ssm_tables_kernel,
        grid=(n,),
        in_specs=[spec(a) for a in args],
        out_specs=[spec(s) for s in out_shape],
        out_shape=out_shape,
        compiler_params=_cparams(("arbitrary",)),
        name="ssm_tables",
    )(*args)


def _ssm_kernel(u_ref, klag_ref, win_ref, wout_ref, apr_ref, api_ref, d_ref, s_ref,
                mt_ref, hin_re, hin_im, hst_re, hst_im, *, batch):
    ch, g, p = SSM_CHUNK, SSM_GROUP, SSM_STATE
    nc = u_ref.shape[0]
    ncb = nc // batch
    causal = (lax.broadcasted_iota(jnp.int32, (ch, ch), 1)
              >= lax.broadcasted_iota(jnp.int32, (ch, ch), 0))

    def build(i, _):
        for o in range(g):
            kv = klag_ref[0, pl.ds(i * g + o, 1), :]
            tz = pltpu.roll(jnp.broadcast_to(kv, (ch, ch)), 0, 1, stride=1, stride_axis=0)
            mt_ref[pl.ds(pl.multiple_of(i * ch, ch), ch), o * ch:(o + 1) * ch] = (
                jnp.where(causal, tz, 0.0).astype(BF16))
        return 0

    lax.fori_loop(0, g, build, 0)

    u2 = u_ref.reshape(nc * g, LANES)
    s2 = s_ref.reshape(nc * g, LANES)
    lhs = jnp.concatenate([u2[pl.ds(i, nc, stride=g), :].astype(BF16) for i in range(g)], axis=1)
    y = _mm(lhs, mt_ref[...])
    hin = _mm(lhs, win_ref[0])
    hin_re[...] = hin[:, :p]
    hin_im[...] = hin[:, p:]
    a_re, a_im = apr_ref[0], api_ref[0]

    def scan(c, carry):
        new = []
        for b in range(batch):
            h_re, h_im = carry[b]
            r = b * ncb + c
            hst_re[pl.ds(r, 1), :] = h_re
            hst_im[pl.ds(r, 1), :] = h_im
            x_re = hin_re[pl.ds(r, 1), :]
            x_im = hin_im[pl.ds(r, 1), :]
            new.append((a_re * h_re - a_im * h_im + x_re, a_re * h_im + a_im * h_re + x_im))
        return tuple(new)

    zero = jnp.zeros((1, p), F32)
    lax.fori_loop(0, ncb, scan, tuple((zero, zero) for _ in range(batch)))
    y = y + _mm(hst_re[...].astype(BF16), wout_ref[0, :p, :]) + _mm(hst_im[...].astype(BF16), wout_ref[0, p:, :])
    for o in range(g):
        yo = y[:, o * ch:(o + 1) * ch] + u2[pl.ds(o, nc, stride=g), :] * d_ref[0, o:o + 1, :]
        s2[pl.ds(o, nc, stride=g), :] = jax.nn.gelu(yo)


def _ssm(ut, klag, win, wout, apr, api, d_skip, layer, batch):
    nc = ut.shape[0]
    g, ch, p = SSM_GROUP, SSM_CHUNK, SSM_STATE
    base = layer * SSM_GROUPS
    tab = lambda a: pl.BlockSpec((1,) + a.shape[1:], lambda i: (base + i, 0, 0))
    kern = functools.partial(_ssm_kernel, batch=batch)
    return pl.pallas_call(
        kern,
        grid=(SSM_GROUPS,),
        in_specs=[pl.BlockSpec((nc, g, LANES), lambda i: (0, i, 0)),
                  tab(klag), tab(win), tab(wout), tab(apr), tab(api),
                  pl.BlockSpec((1, g, 1), lambda i: (i, 0, 0))],
        out_specs=pl.BlockSpec((nc, g, LANES), lambda i: (0, i, 0)),
        out_shape=jax.ShapeDtypeStruct(ut.shape, F32),
        scratch_shapes=[pltpu.VMEM((g * ch, g * ch), BF16),
                        pltpu.VMEM((nc, p), F32), pltpu.VMEM((nc, p), F32),
                        pltpu.VMEM((nc, p), F32), pltpu.VMEM((nc, p), F32)],
        compiler_params=_cparams(("arbitrary",)),
        name="ssm",
    )(ut, klag, win, wout, apr, api, d_skip.reshape(SSM_GROUPS, g, 1))


def _merge_kernel(yt_ref, st_ref, zc_ref, halo_ref, g_ref, x_ref,
                  woa_ref, wglut_ref, bglu_ref, wos_ref, cw_ref, cb_ref, lng_ref, lnb_ref,
                  woc_ref, wout_ref, nf_ref, *rest, tiles_per_seq, with_router):
    if with_router:
        wrh_ref, wrl_ref, xn_ref, h2_ref, route_ref = rest
    else:
        xn_ref, h2_ref = rest
    tm = x_ref.shape[0]
    nch = tm // LANES
    i = pl.program_id(0)

    yt = jnp.concatenate([yt_ref[c] for c in range(nch)], axis=1)
    y_attn = _mm_tn(yt, woa_ref[...])

    st = jnp.concatenate([st_ref[c] for c in range(nch)], axis=1)
    gate = jax.nn.sigmoid(_mm(wglut_ref[...], st.astype(BF16)) + bglu_ref[...])
    y_ssm = _mm_tn((st * gate).astype(BF16), wos_ref[...])

    first = (i % tiles_per_seq) == 0
    zs = jnp.concatenate([jnp.where(first, 0.0, halo_ref[...]), zc_ref[...]], axis=0)
    nz = CONV_HALO + tm
    off = CONV_HALO - (CONV_K - 1)
    conv = cb_ref[...]
    for res in range(SUBLANES):
        zr = zs if res == 0 else pltpu.roll(zs, nz - res, 0)
        for kk in range(CONV_K):
            if (off + kk) % SUBLANES == res:
                lo = off + kk - res
                conv = conv + zr[lo:lo + tm, :] * cw_ref[kk:kk + 1, :]
    mu = jnp.mean(conv, axis=-1, keepdims=True)
    cen = conv - mu
    var = jnp.mean(cen * cen, axis=-1, keepdims=True)
    yc = jax.nn.silu(cen * lax.rsqrt(var + LN_EPS) * lng_ref[...] + lnb_ref[...])
    y_conv = _mm(yc.astype(BF16), woc_ref[...])

    d = D_MODEL
    merged = (g_ref[:, :d].astype(F32) * y_attn + g_ref[:, d:2 * d].astype(F32) * y_ssm
              + g_ref[:, 2 * d:].astype(F32) * y_conv)
    xn = x_ref[...] + _mm(merged.astype(BF16), wout_ref[...])
    xn_ref[...] = xn
    h = _rms(xn, nf_ref[...])
    h2_ref[...] = h.astype(h2_ref.dtype)

    if with_router:
        h_hi = h.astype(BF16)
        h_lo = (h - h_hi.astype(F32)).astype(BF16)
        logits = _mm(h_hi, wrh_ref[...]) + _mm(h_lo, wrh_ref[...]) + _mm(h_hi, wrl_ref[...])
        lane = lax.broadcasted_iota(jnp.int32, (tm, LANES), 1)
        logits = jnp.where(lane < N_EXPERTS, logits, NEG_BIG)
        m1 = jnp.max(logits, axis=-1, keepdims=True)
        i1 = jnp.min(jnp.where(logits == m1, lane, LANES), axis=-1, keepdims=True)
        rest_l = jnp.where(lane == i1, NEG_BIG, logits)
        m2 = jnp.max(rest_l, axis=-1, keepdims=True)
        i2 = jnp.min(jnp.where(rest_l == m2, lane, LANES), axis=-1, keepdims=True)
        e = jnp.exp(m2 - m1)
        g1 = 1.0 / (1.0 + e)
        g2 = e / (1.0 + e)
        route_ref[...] = jnp.where(lane == 0, i1.astype(F32),
                         jnp.where(lane == 1, i2.astype(F32),
                         jnp.where(lane == 2, g1, jnp.where(lane == 3, g2, 0.0))))


def _merge(yt, st, zc, gates, x, w, seq, with_router):
    t = x.shape[0]
    tm = min(TOKEN_TILE, t, seq)
    nch = tm // LANES
    hpt = tm // CONV_HALO
    row = lambda c: pl.BlockSpec((tm, c), lambda i: (i, 0))
    chunked = lambda c: pl.BlockSpec((nch, c, LANES), lambda i: (i, 0, 0))
    weights = [w["woa"], w["wglut"], w["bglu"], w["wos"], w["cw"], w["cb"], w["lng"], w["lnb"],
               w["woc"], w["wout"], w["nf"]]
    out_shape = [jax.ShapeDtypeStruct((t, D_MODEL), F32),
                 jax.ShapeDtypeStruct((t, D_MODEL), F32 if with_router else BF16)]
    out_specs = [row(D_MODEL), row(D_MODEL)]
    if with_router:
        weights += [w["wrh"], w["wrl"]]
        out_shape.append(jax.ShapeDtypeStruct((t, LANES), F32))
        out_specs.append(row(LANES))
    kern = functools.partial(_merge_kernel, tiles_per_seq=seq // tm, with_router=with_router)
    return pl.pallas_call(
        kern,
        grid=(t // tm,),
        in_specs=[chunked(N_HEADS * V_HEAD), chunked(SSM_WIDTH), row(CONV_WIDTH),
                  pl.BlockSpec((CONV_HALO, CONV_WIDTH), lambda i: (jnp.maximum(i * hpt - 1, 0), 0)),
                  row(3 * D_MODEL), row(D_MODEL)] + [_const_spec(a.shape) for a in weights],
        out_specs=out_specs,
        out_shape=out_shape,
        compiler_params=_cparams(("arbitrary",)),
        name="merge_router" if with_router else "merge",
    )(yt, st, zc, zc, gates, x, *weights)


def _ffn_kernel(x_ref, h_ref, w1_ref, w3_ref, w2_ref, o_ref):
    h = h_ref[...]
    f = w1_ref.shape[1]
    acc = x_ref[...]
    for lo in range(0, f, FFN_F_TILE):
        hi = min(lo + FFN_F_TILE, f)
        a = _mm(h, w1_ref[:, lo:hi])
        b = _mm(h, w3_ref[:, lo:hi])
        acc = acc + _mm((jax.nn.silu(a) * b).astype(BF16), w2_ref[lo:hi, :])
    o_ref[...] = acc


def _ffn(x, h, w1, w3, w2):
    t = x.shape[0]
    tm = min(TOKEN_TILE, t)
    row = pl.BlockSpec((tm, D_MODEL), lambda i: (i, 0))
    return pl.pallas_call(
        _ffn_kernel,
        grid=(t // tm,),
        in_specs=[row, row, _const_spec(w1.shape), _const_spec(w3.shape), _const_spec(w2.shape)],
        out_specs=row,
        out_shape=jax.ShapeDtypeStruct((t, D_MODEL), F32),
        compiler_params=_cparams(("arbitrary",)),
        name="ffn",
    )(x, h, w1, w3, w2)


def _dispatch_kernel(dest_ref, h_ref, xs_in_ref, xs_ref, sem):
    del xs_in_ref
    tm = h_ref.shape[0]

    def row_copy(t, k):
        return pltpu.make_async_copy(h_ref.at[pl.ds(t, 1), :],
                                     xs_ref.at[pl.ds(dest_ref[0, 0, TOP_K * t + k], 1), :], sem)

    def issue(t, c):
        for k in range(TOP_K):
            row_copy(t, k).start(priority=k)
        return c

    def drain(t, c):
        for k in range(TOP_K):
            row_copy(t, k).wait()
        return c

    lax.fori_loop(0, tm, issue, 0, unroll=ROW_DMA_UNROLL)
    lax.fori_loop(0, tm, drain, 0, unroll=ROW_DMA_UNROLL)


def _dispatch(h, dest, rows):
    t = h.shape[0]
    tm = min(ROUTE_TILE, t)
    nt = t // tm
    xs0 = jnp.zeros((rows, D_MODEL), h.dtype)
    return pl.pallas_call(
        _dispatch_kernel,
        grid=(nt,),
        in_specs=[pl.BlockSpec((1, 1, TOP_K * tm), lambda i: (i, 0, 0), memory_space=pltpu.SMEM),
                  pl.BlockSpec((tm, D_MODEL), lambda i: (i, 0)),
                  pl.BlockSpec(memory_space=pl.ANY)],
        out_specs=pl.BlockSpec(memory_space=pl.ANY),
        out_shape=jax.ShapeDtypeStruct((rows, D_MODEL), h.dtype),
        scratch_shapes=[pltpu.SemaphoreType.DMA(())],
        input_output_aliases={2: 0},
        compiler_params=_cparams(("arbitrary",)),
        name="moe_dispatch",
    )(dest.reshape(nt, 1, TOP_K * tm), h, xs0)


def _experts_kernel(te_ref, nu_ref, x_ref, w1_ref, w3_ref, w2_ref, y_ref, xb_ref):
    i = pl.program_id(0)
    j = pl.program_id(1)
    used = i < nu_ref[0]

    @pl.when(jnp.logical_and(used, j == 0))
    def _():
        xb_ref[...] = x_ref[...].astype(BF16)

    @pl.when(jnp.logical_and(jnp.logical_not(used), j == 0))
    def _():
        y_ref[...] = jnp.zeros_like(y_ref)

    @pl.when(used)
    def _():
        xb = xb_ref[...]
        a = _mm(xb, w1_ref[0])
        b = _mm(xb, w3_ref[0])
        part = _mm((jax.nn.silu(a) * b).astype(BF16), w2_ref[0])

        @pl.when(j == 0)
        def _():
            y_ref[...] = part

        @pl.when(j > 0)
        def _():
            y_ref[...] += part


def _experts(xs, tile_expert, n_used, w1, w3, w2):
    rows = xs.shape[0]
    tm = MOE_ROW_TILE
    f = w1.shape[2]
    tf = min(MOE_F_TILE, f)
    nf = f // tf

    def fchunk(i, j, nu):
        return jnp.where(i < nu[0], j, nf - 1)

    grid_spec = pltpu.PrefetchScalarGridSpec(
        num_scalar_prefetch=2,
        grid=(rows // tm, nf),
        in_specs=[pl.BlockSpec((tm, D_MODEL), lambda i, j, te, nu: (i, 0)),
                  pl.BlockSpec((1, D_MODEL, tf), lambda i, j, te, nu: (te[i], 0, fchunk(i, j, nu))),
                  pl.BlockSpec((1, D_MODEL, tf), lambda i, j, te, nu: (te[i], 0, fchunk(i, j, nu))),
                  pl.BlockSpec((1, tf, D_MODEL), lambda i, j, te, nu: (te[i], fchunk(i, j, nu), 0))],
        out_specs=pl.BlockSpec((tm, D_MODEL), lambda i, j, te, nu: (i, 0)),
        scratch_shapes=[pltpu.VMEM((tm, D_MODEL), BF16)],
    )
    return pl.pallas_call(
        _experts_kernel,
        grid_spec=grid_spec,
        out_shape=jax.ShapeDtypeStruct((rows, D_MODEL), F32),
        compiler_params=_cparams(("arbitrary", "arbitrary")),
        name="moe_experts",
    )(tile_expert, n_used, xs, w1, w3, w2)


def _combine_kernel(dest_ref, x_ref, route_ref, nfin_ref, ys_ref, o_ref, buf_ref, sem):
    tm = x_ref.shape[0]

    def row_copy(t, k):
        return pltpu.make_async_copy(ys_ref.at[pl.ds(dest_ref[0, 0, TOP_K * t + k], 1), :],
                                     buf_ref.at[pl.ds(k * tm + t, 1), :], sem)

    def issue(t, c):
        for k in range(TOP_K):
            row_copy(t, k).start(priority=k)
        return c

    def drain(t, c):
        for k in range(TOP_K):
            row_copy(t, k).wait()
        return c

    lax.fori_loop(0, tm, issue, 0, unroll=ROW_DMA_UNROLL)
    lax.fori_loop(0, tm, drain, 0, unroll=ROW_DMA_UNROLL)
    route = route_ref[...]
    y = x_ref[...] + route[:, 2:3] * buf_ref[:tm, :] + route[:, 3:4] * buf_ref[tm:, :]
    o_ref[...] = _rms(y, nfin_ref[...])


def _combine(x, route, dest, ys, norm_final):
    t = x.shape[0]
    tm = min(ROUTE_TILE, t)
    nt = t // tm
    return pl.pallas_call(
        _combine_kernel,
        grid=(nt,),
        in_specs=[pl.BlockSpec((1, 1, TOP_K * tm), lambda i: (i, 0, 0), memory_space=pltpu.SMEM),
                  pl.BlockSpec((tm, D_MODEL), lambda i: (i, 0)),
                  pl.BlockSpec((tm, LANES), lambda i: (i, 0)),
                  _const_spec((1, D_MODEL)),
                  pl.BlockSpec(memory_space=pl.ANY)],
        out_specs=pl.BlockSpec((tm, D_MODEL), lambda i: (i, 0)),
        out_shape=jax.ShapeDtypeStruct((t, D_MODEL), F32),
        scratch_shapes=[pltpu.VMEM((TOP_K * tm, D_MODEL), F32), pltpu.SemaphoreType.DMA(())],
        compiler_params=_cparams(("arbitrary",)),
        name="moe_combine",
    )(dest.reshape(nt, 1, TOP_K * tm), x, route, norm_final.reshape(1, D_MODEL), ys)


def _moe_plan(route, row_tile):
    t = route.shape[0]
    m = t * TOP_K
    e_flat = route[:, :TOP_K].astype(jnp.int32).reshape(m)
    onehot = (e_flat[:, None] == jnp.arange(N_EXPERTS, dtype=jnp.int32)[None, :]).astype(jnp.int32)
    csum = jnp.cumsum(onehot, axis=0)
    rank = jnp.sum((csum - onehot) * onehot, axis=1)
    counts = csum[-1]
    padded = ((counts + row_tile - 1) // row_tile) * row_tile
    pad_ends = jnp.cumsum(padded)
    pad_starts = pad_ends - padded
    dest = jnp.sum(onehot * pad_starts[None, :], axis=1) + rank
    n_tiles = -(-m // row_tile) + N_EXPERTS
    tile_start = jnp.arange(n_tiles, dtype=jnp.int32) * row_tile
    tile_expert = jnp.minimum(jnp.sum((pad_ends[None, :] <= tile_start[:, None]).astype(jnp.int32), axis=1),
                              N_EXPERTS - 1)
    n_used = (pad_ends[-1] // row_tile).reshape(1)
    return dest.astype(jnp.int32), tile_expert.astype(jnp.int32), n_used.astype(jnp.int32), n_tiles * row_tile


def _prep_mixer_weights(l, w_in, q_norm, w_uq, kv_norm, w_ukv, w_o_attn, ssm_w_glu, ssm_b_glu, w_o_ssm,
                        conv_w, conv_b, conv_ln_g, conv_ln_b, w_o_conv, w_out, norm_ffn):
    o = IN_OFFSETS
    wi = w_in[l]
    low = wi[:, o[0]:o[3]]
    wa = jnp.pad(low, ((0, 0), (0, 4 * LANES - low.shape[1]))).astype(BF16)
    wq = jnp.pad(w_uq[l].reshape(Q_LORA, N_HEADS, QK_HEAD),
                 ((0, 0), (0, 0), (0, HEAD_PAD - QK_HEAD))).reshape(Q_LORA, N_HEADS * HEAD_PAD).astype(BF16)
    wkv = w_ukv[l].reshape(KV_LORA, N_HEADS, QK_NOPE + V_HEAD)
    wkn = jnp.pad(wkv[..., :QK_NOPE], ((0, 0), (0, 0), (0, HEAD_PAD - QK_NOPE))
                  ).reshape(KV_LORA, N_HEADS * HEAD_PAD).astype(BF16)
    wvt = wkv[..., QK_NOPE:].reshape(KV_LORA, N_HEADS * V_HEAD).T.astype(BF16)
    place = np.zeros((LANES, N_HEADS * HEAD_PAD), np.float32)
    for h in range(N_HEADS):
        for r in range(QK_ROPE):
            place[r, h * HEAD_PAD + QK_NOPE + r] = 1.0
    return dict(
        wa=wa, wut=wi[:, o[3]:o[4]].T.astype(BF16), wca=wi[:, o[4]:o[5]].astype(BF16),
        wcg=wi[:, o[5]:o[6]].astype(BF16), wg=wi[:, o[6]:o[7]].astype(BF16),
        qn=q_norm[l].reshape(1, Q_LORA), kvn=kv_norm[l].reshape(1, KV_LORA),
        wq=wq, wkn=wkn, wke=jnp.asarray(place, BF16), wvt=wvt,
        woa=w_o_attn[l].astype(BF16), wglut=ssm_w_glu[l].T.astype(BF16),
        bglu=ssm_b_glu[l].reshape(SSM_WIDTH, 1), wos=w_o_ssm[l].astype(BF16),
        cw=conv_w[l], cb=conv_b[l].reshape(1, CONV_WIDTH), lng=conv_ln_g[l].reshape(1, CONV_WIDTH),
        lnb=conv_ln_b[l].reshape(1, CONV_WIDTH), woc=w_o_conv[l].astype(BF16),
        wout=w_out[l].astype(BF16), nf=norm_ffn[l].reshape(1, D_MODEL))


def kernel(x, positions, norm_mix, w_in, q_norm, w_uq, kv_norm, w_ukv, w_o_attn, ssm_lam_re, ssm_lam_im, ssm_log_dt, ssm_b_re, ssm_b_im, ssm_c_re, ssm_c_im, ssm_d, ssm_w_glu, ssm_b_glu, w_o_ssm, conv_w, conv_b, conv_ln_g, conv_ln_b, w_o_conv, w_out, norm_ffn, ffn_w1, ffn_w3, ffn_w2, moe_router, moe_w1, moe_w3, moe_w2, norm_final):
    batch, seq, d = x.shape
    depth = w_in.shape[0]
    assert d == D_MODEL and depth == 2 and seq % SSM_CHUNK == 0
    t = batch * seq
    xf = x.reshape(t, d)
    cos_t, sin_t = _rope_tables(positions)
    ng = depth * SSM_GROUPS
    klag, win, wout_s, apr, api = _ssm_tables(
        ssm_lam_re.reshape(ng, SSM_STATE), ssm_lam_im.reshape(ng, SSM_STATE), ssm_log_dt.reshape(ng),
        ssm_b_re.reshape(ng, SSM_STATE, SSM_GROUP), ssm_b_im.reshape(ng, SSM_STATE, SSM_GROUP),
        ssm_c_re.reshape(ng, SSM_GROUP, SSM_STATE), ssm_c_im.reshape(ng, SSM_GROUP, SSM_STATE))

    for layer in range(depth):
        moe_layer = layer % 2 == 1
        w = _prep_mixer_weights(layer, w_in, q_norm, w_uq, kv_norm, w_ukv, w_o_attn, ssm_w_glu, ssm_b_glu,
                                w_o_ssm, conv_w, conv_b, conv_ln_g, conv_ln_b, w_o_conv, w_out, norm_ffn)
        q, k, vt, ut, zc, gates = _mixer_in(xf, norm_mix[layer].reshape(1, d), w, cos_t, sin_t)
        yt = _attention(q, k, vt, batch, seq)
        st = _ssm(ut, klag, win, wout_s, apr, api, ssm_d[layer], layer, batch)
        i = layer // 2
        if not moe_layer:
            xn, h2 = _merge(yt, st, zc, gates, xf, w, seq, with_router=False)
            xf = _ffn(xn, h2, ffn_w1[i].astype(BF16), ffn_w3[i].astype(BF16), ffn_w2[i].astype(BF16))
        else:
            wr = jnp.pad(moe_router[i], ((0, 0), (0, LANES - N_EXPERTS)))
            w["wrh"] = wr.astype(BF16)
            w["wrl"] = (wr - w["wrh"].astype(F32)).astype(BF16)
            xn, h2, route = _merge(yt, st, zc, gates, xf, w, seq, with_router=True)
            dest, tile_expert, n_used, rows = _moe_plan(route, MOE_ROW_TILE)
            xs = _dispatch(h2, dest, rows)
            ys = _experts(xs, tile_expert, n_used, moe_w1[i].astype(BF16), moe_w3[i].astype(BF16),
                          moe_w2[i].astype(BF16))
            xf = _combine(xn, route, dest, ys, norm_final)
    return xf.reshape(batch, seq, d)
```

```python
import functools

import numpy as np
import jax
import jax.numpy as jnp
from jax import lax
from jax.experimental import pallas as pl
from jax.experimental.pallas import tpu as pltpu

F32 = jnp.float32
BF16 = jnp.bfloat16

D_MODEL = 1024
N_HEADS = 8
Q_LORA = 256
KV_LORA = 128
QK_NOPE = 64
QK_ROPE = 32
QK_HEAD = QK_NOPE + QK_ROPE
V_HEAD = 64
ROPE_THETA = 10000.0
SSM_WIDTH = 512
SSM_GROUP = 16
SSM_GROUPS = SSM_WIDTH // SSM_GROUP
SSM_STATE = 64
SSM_CHUNK = 128
CONV_WIDTH = 512
CONV_K = 31
N_EXPERTS = 8
TOP_K = 2
RMS_EPS = 1e-6
LN_EPS = 1e-5
IN_SPLITS = (Q_LORA, KV_LORA, QK_ROPE, SSM_WIDTH, CONV_WIDTH, CONV_WIDTH, 3 * D_MODEL)
IN_OFFSETS = tuple(int(v) for v in np.cumsum((0,) + IN_SPLITS))

LANES = 128
SUBLANES = 8
HEAD_PAD = LANES
VMEM_LIMIT_BYTES = 56 * 1024 * 1024

TOKEN_TILE = 512
ATTN_TILE = 1024
ATTN_COLS = 256
ATTN_ROWS = 256
ATTN_UNROLL = 4
MOE_ROW_TILE = 1024
MOE_F_TILE = 512
FFN_F_TILE = 512
ROUTE_TILE = 256
ROW_DMA_UNROLL = 8
CONV_HALO = 32
SUM_ROWS = 16
NEG_BIG = -1e30
LOG2_E = float(np.log2(np.e))


def _cparams(sem):
    return pltpu.CompilerParams(dimension_semantics=sem, vmem_limit_bytes=VMEM_LIMIT_BYTES)


def _const_spec(shape):
    nd = len(shape)
    return pl.BlockSpec(shape, lambda *_: (0,) * nd, pipeline_mode=pl.Buffered(1))


def _rms(x, g):
    return x * lax.rsqrt(jnp.mean(x * x, axis=-1, keepdims=True) + RMS_EPS) * g


def _mm(a, b):
    return jnp.dot(a, b, preferred_element_type=F32)


def _mm_nt(a, b):
    return lax.dot_general(a, b, (((1,), (1,)), ((), ())), preferred_element_type=F32)


def _mm_tn(a, b):
    return lax.dot_general(a, b, (((0,), (0,)), ((), ())), preferred_element_type=F32)


def _rope_tables_kernel(pos_ref, invf_ref, sign_ref, cos_ref, sin_ref):
    ang = pos_ref[...].astype(F32) * invf_ref[...]
    cos_ref[...] = jnp.cos(ang)
    sin_ref[...] = jnp.sin(ang) * sign_ref[...]


def _rope_tables(positions):
    t = positions.size
    tm = min(2048, t)
    half = QK_ROPE // 2
    inv_freq = ROPE_THETA ** (-np.arange(0, QK_ROPE, 2, dtype=np.float32) / QK_ROPE)
    invf = np.zeros((1, LANES), np.float32)
    sign = np.zeros((1, LANES), np.float32)
    invf[0, QK_NOPE:QK_NOPE + half] = inv_freq
    invf[0, QK_NOPE + half:QK_HEAD] = inv_freq
    sign[0, QK_NOPE:QK_NOPE + half] = -1.0
    sign[0, QK_NOPE + half:QK_HEAD] = 1.0
    return pl.pallas_call(
        _rope_tables_kernel,
        grid=(t // tm,),
        in_specs=[pl.BlockSpec((tm, 1), lambda i: (i, 0)),
                  pl.BlockSpec((1, LANES), lambda i: (0, 0)),
                  pl.BlockSpec((1, LANES), lambda i: (0, 0))],
        out_specs=[pl.BlockSpec((tm, LANES), lambda i: (i, 0))] * 2,
        out_shape=[jax.ShapeDtypeStruct((t, LANES), F32)] * 2,
        compiler_params=_cparams(("arbitrary",)),
        name="rope_tables",
    )(positions.reshape(t, 1), jnp.asarray(invf), jnp.asarray(sign))


def _rope_head(xh, cos, sin, lane):
    half = QK_ROPE // 2
    swapped = jnp.where(lane < QK_NOPE + half,
                        pltpu.roll(xh, LANES - half, 1), pltpu.roll(xh, half, 1))
    return xh * cos + swapped * sin


def _mixer_in_kernel(x_ref, nm_ref, wa_ref, wut_ref, wca_ref, wcg_ref, wg_ref, qn_ref, kvn_ref,
                     wq_ref, wkn_ref, wke_ref, wvt_ref, cos_ref, sin_ref,
                     q_ref, k_ref, vt_ref, ut_ref, zc_ref, g_ref):
    tm = x_ref.shape[0]
    hn = _rms(x_ref[...], nm_ref[...]).astype(BF16)
    za = _mm(hn, wa_ref[...])
    cqn = _rms(za[:, :Q_LORA], qn_ref[...]).astype(BF16)
    ckvn = _rms(za[:, Q_LORA:Q_LORA + KV_LORA], kvn_ref[...]).astype(BF16)
    kpe = za[:, Q_LORA + KV_LORA:].astype(BF16)
    q = _mm(cqn, wq_ref[...])
    k = _mm(ckvn, wkn_ref[...]) + _mm(kpe, wke_ref[...])
    cos = cos_ref[...]
    sin = sin_ref[...]
    lane = lax.broadcasted_iota(jnp.int32, (tm, LANES), 1)
    scale = QK_HEAD ** -0.5 * LOG2_E
    for h in range(N_HEADS):
        sl = slice(h * HEAD_PAD, (h + 1) * HEAD_PAD)
        q_ref[:, sl] = (_rope_head(q[:, sl], cos, sin, lane) * scale).astype(BF16)
        k_ref[:, sl] = _rope_head(k[:, sl], cos, sin, lane).astype(BF16)
    vt = _mm_nt(wvt_ref[...], ckvn)
    ut = _mm_nt(wut_ref[...], hn)
    for c in range(tm // LANES):
        sl = slice(c * LANES, (c + 1) * LANES)
        vt_ref[c] = vt[:, sl].astype(BF16)
        ut_ref[c] = ut[:, sl]
    zc_ref[...] = _mm(hn, wca_ref[...]) * jax.nn.sigmoid(_mm(hn, wcg_ref[...]))
    for c in range(3):
        sl = slice(c * D_MODEL, (c + 1) * D_MODEL)
        g_ref[:, sl] = jax.nn.sigmoid(_mm(hn, wg_ref[:, sl])).astype(BF16)


def _mixer_in(x, nm, w, cos_t, sin_t):
    t = x.shape[0]
    tm = min(TOKEN_TILE, t)
    nch = tm // LANES
    row = lambda c: pl.BlockSpec((tm, c), lambda i: (i, 0))
    chunked = lambda c: pl.BlockSpec((nch, c, LANES), lambda i: (i, 0, 0))
    weights = [nm, w["wa"], w["wut"], w["wca"], w["wcg"], w["wg"], w["qn"], w["kvn"],
               w["wq"], w["wkn"], w["wke"], w["wvt"]]
    return pl.pallas_call(
        _mixer_in_kernel,
        grid=(t // tm,),
        in_specs=[row(D_MODEL)] + [_const_spec(a.shape) for a in weights] + [row(LANES), row(LANES)],
        out_specs=[row(N_HEADS * HEAD_PAD), row(N_HEADS * HEAD_PAD), chunked(N_HEADS * V_HEAD),
                   chunked(SSM_WIDTH), row(CONV_WIDTH), row(3 * D_MODEL)],
        out_shape=[jax.ShapeDtypeStruct((t, N_HEADS * HEAD_PAD), BF16),
                   jax.ShapeDtypeStruct((t, N_HEADS * HEAD_PAD), BF16),
                   jax.ShapeDtypeStruct((t // LANES, N_HEADS * V_HEAD, LANES), BF16),
                   jax.ShapeDtypeStruct((t // LANES, SSM_WIDTH, LANES), F32),
                   jax.ShapeDtypeStruct((t, CONV_WIDTH), F32),
                   jax.ShapeDtypeStruct((t, 3 * D_MODEL), BF16)],
        compiler_params=_cparams(("arbitrary",)),
        name="mixer_in",
    )(x, *weights, cos_t, sin_t)


def _attn_kernel(q_ref, k_ref, vt_ref, o_ref, sa_ref, sb_ref, mxa_ref, mxb_ref, m_ref, acc_ref, *, tq, tk):
    qi = pl.program_id(2)
    cw = ATTN_COLS
    ncol = tq // cw
    nkc = tk // LANES
    rw = min(ATTN_ROWS, tk)
    ones_rows = (lax.broadcasted_iota(jnp.int32, (SUM_ROWS, rw), 0) == 0).astype(BF16)
    ping = (sa_ref, mxa_ref)
    pong = (sb_ref, mxb_ref)

    def score_pass(j, c, dst, masked):
        s_ref, mx_ref = dst
        qc = q_ref[pl.ds(pl.multiple_of(c * cw, cw), cw), :]
        mx = None
        for r in range(tk // rw):
            kb = k_ref[pl.ds(pl.multiple_of(j * tk + r * rw, rw), rw), :]
            st = _mm_nt(kb, qc)
            if masked:
                kpos = r * rw + lax.broadcasted_iota(jnp.int32, (rw, cw), 0)
                qpos = c * cw + lax.broadcasted_iota(jnp.int32, (rw, cw), 1)
                st = jnp.where(kpos <= qpos, st, NEG_BIG)
            s_ref[c, r * rw:(r + 1) * rw, :] = st
            mr = jnp.max(st, axis=0, keepdims=True)
            mx = mr if mx is None else jnp.maximum(mx, mr)
        mx_ref[c] = mx

    def value_pass(j, c, src):
        s_ref, mx_ref = src
        m_old = m_ref[c]
        m_new = jnp.maximum(m_old, mx_ref[c])
        pv = None
        for r in range(tk // rw):
            p = jnp.exp2(s_ref[c, r * rw:(r + 1) * rw, :] - m_new).astype(BF16)
            vb = jnp.concatenate([vt_ref[j * nkc + r * (rw // LANES) + i] for i in range(rw // LANES)], axis=1)
            vb = jnp.concatenate([vb, ones_rows], axis=0)
            part = _mm(vb, p)
            pv = part if pv is None else pv + part
        acc_ref[c] = jnp.exp2(m_old - m_new) * acc_ref[c] + pv
        m_ref[c] = m_new

    def over_cols(fn):
        def body(c, carry):
            fn(c)
            return carry
        lax.fori_loop(0, ncol, body, 0, unroll=ATTN_UNROLL)

    def step(j, src, dst, masked):
        def fn(c):
            value_pass(j, c, src)
            score_pass(j + 1, c, dst, masked)
        over_cols(fn)

    m_ref[...] = jnp.full(m_ref.shape, NEG_BIG, F32)
    acc_ref[...] = jnp.zeros(acc_ref.shape, F32)

    @pl.when(qi == 0)
    def _():
        over_cols(lambda c: score_pass(0, c, ping, True))

    @pl.when(qi > 0)
    def _():
        over_cols(lambda c: score_pass(0, c, ping, False))

    n_plain = jnp.maximum(qi - 1, 0)

    def pair(i, carry):
        step(2 * i, ping, pong, False)
        step(2 * i + 1, pong, ping, False)
        return carry

    lax.fori_loop(0, n_plain // 2, pair, 0)

    @pl.when(n_plain % 2 == 1)
    def _():
        step(n_plain - 1, ping, pong, False)

    @pl.when(jnp.logical_and(qi > 0, qi % 2 == 1))
    def _():
        step(qi - 1, ping, pong, True)
        over_cols(lambda c: value_pass(qi, c, pong))

    @pl.when(jnp.logical_and(qi > 0, qi % 2 == 0))
    def _():
        step(qi - 1, pong, ping, True)

    @pl.when(qi % 2 == 0)
    def _():
        over_cols(lambda c: value_pass(qi, c, ping))

    for c in range(ncol):
        acc = acc_ref[c]
        out = acc[:V_HEAD] / acc[V_HEAD:V_HEAD + 1]
        for i in range(cw // LANES):
            o_ref[c * (cw // LANES) + i] = out[:, i * LANES:(i + 1) * LANES].astype(BF16)


def _attention(q, k, vt, batch, seq):
    tq = tk = min(ATTN_TILE, seq)
    nq = seq // tq
    ncol = tq // ATTN_COLS
    kern = functools.partial(_attn_kernel, tq=tq, tk=tk)
    return pl.pallas_call(
        kern,
        grid=(batch, N_HEADS, nq),
        in_specs=[pl.BlockSpec((tq, HEAD_PAD), lambda b, h, i: (b * nq + i, h)),
                  pl.BlockSpec((seq, HEAD_PAD), lambda b, h, i: (b, h)),
                  pl.BlockSpec((seq // LANES, V_HEAD, LANES), lambda b, h, i: (b, h, 0))],
        out_specs=pl.BlockSpec((tq // LANES, V_HEAD, LANES), lambda b, h, i: (b * nq + i, h, 0)),
        out_shape=jax.ShapeDtypeStruct((batch * seq // LANES, N_HEADS * V_HEAD, LANES), BF16),
        scratch_shapes=[pltpu.VMEM((ncol, tk, ATTN_COLS), F32),
                        pltpu.VMEM((ncol, tk, ATTN_COLS), F32),
                        pltpu.VMEM((ncol, 1, ATTN_COLS), F32),
                        pltpu.VMEM((ncol, 1, ATTN_COLS), F32),
                        pltpu.VMEM((ncol, 1, ATTN_COLS), F32),
                        pltpu.VMEM((ncol, V_HEAD + SUM_ROWS, ATTN_COLS), F32)],
        compiler_params=_cparams(("arbitrary", "arbitrary", "arbitrary")),
        name="attention",
    )(q, k, vt)


def _ssm_tables_kernel(lrr_ref, lir_ref, lrc_ref, lic_ref, ldt_ref, btr_ref, bti_ref,
                       cr_ref, ci_ref, ctr_ref, cti_ref,
                       klag_ref, win_ref, wout_ref, apr_ref, api_ref):
    ch = SSM_CHUNK
    dt = jnp.exp(ldt_ref[0])

    def power(lr, li, n):
        mag = jnp.exp(lr * dt * n)
        return mag * jnp.cos(li * dt * n), mag * jnp.sin(li * dt * n)

    lr = jnp.minimum(lrr_ref[0], -1e-4)
    li = lir_ref[0]
    a_re, a_im = power(lr, li, 1.0)
    den = lr * lr + li * li
    nr, ni = a_re - 1.0, a_im
    coef_re = (nr * lr + ni * li) / den
    coef_im = (ni * lr - nr * li) / den
    btr, bti = btr_ref[0], bti_ref[0]
    bbr = coef_re * btr - coef_im * bti
    bbi = coef_re * bti + coef_im * btr
    cr, ci = cr_ref[0], ci_ref[0]
    f_re = jnp.concatenate([cr * bbr[i:i + 1] - ci * bbi[i:i + 1] for i in range(SSM_GROUP)], axis=0)
    f_im = jnp.concatenate([cr * bbi[i:i + 1] + ci * bbr[i:i + 1] for i in range(SSM_GROUP)], axis=0)

    lrc = jnp.minimum(lrc_ref[0], -1e-4)
    lic = lic_ref[0]
    lag = lax.broadcasted_iota(jnp.int32, (1, ch), 1).astype(F32)
    pl_re, pl_im = power(lrc, lic, lag)
    hi = lax.Precision.HIGHEST
    klag_ref[0] = (jnp.dot(f_re, pl_re, precision=hi, preferred_element_type=F32)
                   - jnp.dot(f_im, pl_im, precision=hi, preferred_element_type=F32))

    back = (ch - 1) - lax.broadcasted_iota(jnp.int32, (ch, 1), 0).astype(F32)
    q_re, q_im = power(lr, li, back)
    for i in range(SSM_GROUP):
        w_re = q_re * bbr[i:i + 1] - q_im * bbi[i:i + 1]
        w_im = q_re * bbi[i:i + 1] + q_im * bbr[i:i + 1]
        win_ref[0, i * ch:(i + 1) * ch, :] = jnp.concatenate([w_re, w_im], axis=1).astype(BF16)

    pw_re, pw_im = power(lrc, lic, lag + 1.0)
    ctr, cti = ctr_ref[0], cti_ref[0]
    p = SSM_STATE
    for o in range(SSM_GROUP):
        c_re, c_im = ctr[:, o:o + 1], cti[:, o:o + 1]
        wout_ref[0, :p, o * ch:(o + 1) * ch] = (c_re * pw_re - c_im * pw_im).astype(BF16)
        wout_ref[0, p:, o * ch:(o + 1) * ch] = (-(c_re * pw_im + c_im * pw_re)).astype(BF16)

    apr_ref[0], api_ref[0] = power(lr, li, float(ch))


def _ssm_tables(lam_re, lam_im, log_dt, b_re, b_im, c_re, c_im):
    n = lam_re.shape[0]
    p, g, ch = SSM_STATE, SSM_GROUP, SSM_CHUNK
    args = [lam_re.reshape(n, 1, p), lam_im.reshape(n, 1, p),
            lam_re.reshape(n, p, 1), lam_im.reshape(n, p, 1), log_dt.reshape(n, 1, 1),
            jnp.swapaxes(b_re, 1, 2), jnp.swapaxes(b_im, 1, 2), c_re, c_im,
            jnp.swapaxes(c_re, 1, 2), jnp.swapaxes(c_im, 1, 2)]
    spec = lambda a: pl.BlockSpec((1,) + a.shape[1:], lambda i: (i, 0, 0))
    out_shape = [jax.ShapeDtypeStruct((n, g * g, ch), F32),
                 jax.ShapeDtypeStruct((n, g * ch, 2 * p), BF16),
                 jax.ShapeDtypeStruct((n, 2 * p, g * ch), BF16),
                 jax.ShapeDtypeStruct((n, 1, p), F32),
                 jax.ShapeDtypeStruct((n, 1, p), F32)]
    return pl.pallas_call(
        _ssm_tables_kernel,
        grid=(n,),
        in_specs=[spec(a) for a in args],
        out_specs=[spec(s) for s in out_shape],
        out_shape=out_shape,
        compiler_params=_cparams(("arbitrary",)),
        name="ssm_tables",
    )(*args)


def _ssm_kernel(u_ref, klag_ref, win_ref, wout_ref, apr_ref, api_ref, d_ref, s_ref,
                mt_ref, hin_re, hin_im, hst_re, hst_im, *, batch):
    ch, g, p = SSM_CHUNK, SSM_GROUP, SSM_STATE
    nc = u_ref.shape[0]
    ncb = nc // batch
    causal = (lax.broadcasted_iota(jnp.int32, (ch, ch), 1)
              >= lax.broadcasted_iota(jnp.int32, (ch, ch), 0))

    def build(i, _):
        for o in range(g):
            kv = klag_ref[0, pl.ds(i * g + o, 1), :]
            tz = pltpu.roll(jnp.broadcast_to(kv, (ch, ch)), 0, 1, stride=1, stride_axis=0)
            mt_ref[pl.ds(pl.multiple_of(i * ch, ch), ch), o * ch:(o + 1) * ch] = (
                jnp.where(causal, tz, 0.0).astype(BF16))
        return 0

    lax.fori_loop(0, g, build, 0)

    u2 = u_ref.reshape(nc * g, LANES)
    s2 = s_ref.reshape(nc * g, LANES)
    lhs = jnp.concatenate([u2[pl.ds(i, nc, stride=g), :].astype(BF16) for i in range(g)], axis=1)
    y = _mm(lhs, mt_ref[...])
    hin = _mm(lhs, win_ref[0])
    hin_re[...] = hin[:, :p]
    hin_im[...] = hin[:, p:]
    a_re, a_im = apr_ref[0], api_ref[0]

    def scan(c, carry):
        new = []
        for b in range(batch):
            h_re, h_im = carry[b]
            r = b * ncb + c
            hst_re[pl.ds(r, 1), :] = h_re
            hst_im[pl.ds(r, 1), :] = h_im
            x_re = hin_re[pl.ds(r, 1), :]
            x_im = hin_im[pl.ds(r, 1), :]
            new.append((a_re * h_re - a_im * h_im + x_re, a_re * h_im + a_im * h_re + x_im))
        return tuple(new)

    zero = jnp.zeros((1, p), F32)
    lax.fori_loop(0, ncb, scan, tuple((zero, zero) for _ in range(batch)))
    y = y + _mm(hst_re[...].astype(BF16), wout_ref[0, :p, :]) + _mm(hst_im[...].astype(BF16), wout_ref[0, p:, :])
    for o in range(g):
        yo = y[:, o * ch:(o + 1) * ch] + u2[pl.ds(o, nc, stride=g), :] * d_ref[0, o:o + 1, :]
        s2[pl.ds(o, nc, stride=g), :] = jax.nn.gelu(yo)


def _ssm(ut, klag, win, wout, apr, api, d_skip, layer, batch):
    nc = ut.shape[0]
    g, ch, p = SSM_GROUP, SSM_CHUNK, SSM_STATE
    base = layer * SSM_GROUPS
    tab = lambda a: pl.BlockSpec((1,) + a.shape[1:], lambda i: (base + i, 0, 0))
    kern = functools.partial(_ssm_kernel, batch=batch)
    return pl.pallas_call(
        kern,
        grid=(SSM_GROUPS,),
        in_specs=[pl.BlockSpec((nc, g, LANES), lambda i: (0, i, 0)),
                  tab(klag), tab(win), tab(wout), tab(apr), tab(api),
                  pl.BlockSpec((1, g, 1), lambda i: (i, 0, 0))],
        out_specs=pl.BlockSpec((nc, g, LANES), lambda i: (0, i, 0)),
        out_shape=jax.ShapeDtypeStruct(ut.shape, F32),
        scratch_shapes=[pltpu.VMEM((g * ch, g * ch), BF16),
                        pltpu.VMEM((nc, p), F32), pltpu.VMEM((nc, p), F32),
                        pltpu.VMEM((nc, p), F32), pltpu.VMEM((nc, p), F32)],
        compiler_params=_cparams(("arbitrary",)),
        name="ssm",
    )(ut, klag, win, wout, apr, api, d_skip.reshape(SSM_GROUPS, g, 1))


def _merge_kernel(yt_ref, st_ref, zc_ref, halo_ref, g_ref, x_ref,
                  woa_ref, wglut_ref, bglu_ref, wos_ref, cw_ref, cb_ref, lng_ref, lnb_ref,
                  woc_ref, wout_ref, nf_ref, *rest, tiles_per_seq, with_router):
    if with_router:
        wrh_ref, wrl_ref, xn_ref, h2_ref, route_ref = rest
    else:
        xn_ref, h2_ref = rest
    tm = x_ref.shape[0]
    nch = tm // LANES
    i = pl.program_id(0)

    yt = jnp.concatenate([yt_ref[c] for c in range(nch)], axis=1)
    y_attn = _mm_tn(yt, woa_ref[...])

    st = jnp.concatenate([st_ref[c] for c in range(nch)], axis=1)
    gate = jax.nn.sigmoid(_mm(wglut_ref[...], st.astype(BF16)) + bglu_ref[...])
    y_ssm = _mm_tn((st * gate).astype(BF16), wos_ref[...])

    first = (i % tiles_per_seq) == 0
    zs = jnp.concatenate([jnp.where(first, 0.0, halo_ref[...]), zc_ref[...]], axis=0)
    nz = CONV_HALO + tm
    off = CONV_HALO - (CONV_K - 1)
    conv = cb_ref[...]
    for res in range(SUBLANES):
        zr = zs if res == 0 else pltpu.roll(zs, nz - res, 0)
        for kk in range(CONV_K):
            if (off + kk) % SUBLANES == res:
                lo = off + kk - res
                conv = conv + zr[lo:lo + tm, :] * cw_ref[kk:kk + 1, :]
    mu = jnp.mean(conv, axis=-1, keepdims=True)
    cen = conv - mu
    var = jnp.mean(cen * cen, axis=-1, keepdims=True)
    yc = jax.nn.silu(cen * lax.rsqrt(var + LN_EPS) * lng_ref[...] + lnb_ref[...])
    y_conv = _mm(yc.astype(BF16), woc_ref[...])

    d = D_MODEL
    merged = (g_ref[:, :d].astype(F32) * y_attn + g_ref[:, d:2 * d].astype(F32) * y_ssm
              + g_ref[:, 2 * d:].astype(F32) * y_conv)
    xn = x_ref[...] + _mm(merged.astype(BF16), wout_ref[...])
    xn_ref[...] = xn
    h = _rms(xn, nf_ref[...])
    h2_ref[...] = h.astype(h2_ref.dtype)

    if with_router:
        h_hi = h.astype(BF16)
        h_lo = (h - h_hi.astype(F32)).astype(BF16)
        logits = _mm(h_hi, wrh_ref[...]) + _mm(h_lo, wrh_ref[...]) + _mm(h_hi, wrl_ref[...])
        lane = lax.broadcasted_iota(jnp.int32, (tm, LANES), 1)
        logits = jnp.where(lane < N_EXPERTS, logits, NEG_BIG)
        m1 = jnp.max(logits, axis=-1, keepdims=True)
        i1 = jnp.min(jnp.where(logits == m1, lane, LANES), axis=-1, keepdims=True)
        rest_l = jnp.where(lane == i1, NEG_BIG, logits)
        m2 = jnp.max(rest_l, axis=-1, keepdims=True)
        i2 = jnp.min(jnp.where(rest_l == m2, lane, LANES), axis=-1, keepdims=True)
        e = jnp.exp(m2 - m1)
        g1 = 1.0 / (1.0 + e)
        g2 = e / (1.0 + e)
        route_ref[...] = jnp.where(lane == 0, i1.astype(F32),
                         jnp.where(lane == 1, i2.astype(F32),
                         jnp.where(lane == 2, g1, jnp.where(lane == 3, g2, 0.0))))


def _merge(yt, st, zc, gates, x, w, seq, with_router):
    t = x.shape[0]
    tm = min(TOKEN_TILE, t, seq)
    nch = tm // LANES
    hpt = tm // CONV_HALO
    row = lambda c: pl.BlockSpec((tm, c), lambda i: (i, 0))
    chunked = lambda c: pl.BlockSpec((nch, c, LANES), lambda i: (i, 0, 0))
    weights = [w["woa"], w["wglut"], w["bglu"], w["wos"], w["cw"], w["cb"], w["lng"], w["lnb"],
               w["woc"], w["wout"], w["nf"]]
    out_shape = [jax.ShapeDtypeStruct((t, D_MODEL), F32),
                 jax.ShapeDtypeStruct((t, D_MODEL), F32 if with_router else BF16)]
    out_specs = [row(D_MODEL), row(D_MODEL)]
    if with_router:
        weights += [w["wrh"], w["wrl"]]
        out_shape.append(jax.ShapeDtypeStruct((t, LANES), F32))
        out_specs.append(row(LANES))
    kern = functools.partial(_merge_kernel, tiles_per_seq=seq // tm, with_router=with_router)
    return pl.pallas_call(
        kern,
        grid=(t // tm,),
        in_specs=[chunked(N_HEADS * V_HEAD), chunked(SSM_WIDTH), row(CONV_WIDTH),
                  pl.BlockSpec((CONV_HALO, CONV_WIDTH), lambda i: (jnp.maximum(i * hpt - 1, 0), 0)),
                  row(3 * D_MODEL), row(D_MODEL)] + [_const_spec(a.shape) for a in weights],
        out_specs=out_specs,
        out_shape=out_shape,
        compiler_params=_cparams(("arbitrary",)),
        name="merge_router" if with_router else "merge",
    )(yt, st, zc, zc, gates, x, *weights)


def _ffn_kernel(x_ref, h_ref, w1_ref, w3_ref, w2_ref, o_ref):
    h = h_ref[...]
    f = w1_ref.shape[1]
    acc = x_ref[...]
    for lo in range(0, f, FFN_F_TILE):
        hi = min(lo + FFN_F_TILE, f)
        a = _mm(h, w1_ref[:, lo:hi])
        b = _mm(h, w3_ref[:, lo:hi])
        acc = acc + _mm((jax.nn.silu(a) * b).astype(BF16), w2_ref[lo:hi, :])
    o_ref[...] = acc


def _ffn(x, h, w1, w3, w2):
    t = x.shape[0]
    tm = min(TOKEN_TILE, t)
    row = pl.BlockSpec((tm, D_MODEL), lambda i: (i, 0))
    return pl.pallas_call(
        _ffn_kernel,
        grid=(t // tm,),
        in_specs=[row, row, _const_spec(w1.shape), _const_spec(w3.shape), _const_spec(w2.shape)],
        out_specs=row,
        out_shape=jax.ShapeDtypeStruct((t, D_MODEL), F32),
        compiler_params=_cparams(("arbitrary",)),
        name="ffn",
    )(x, h, w1, w3, w2)


def _dispatch_kernel(dest_ref, h_ref, xs_in_ref, xs_ref, sem):
    del xs_in_ref
    tm = h_ref.shape[0]

    def row_copy(t, k):
        return pltpu.make_async_copy(h_ref.at[pl.ds(t, 1), :],
                                     xs_ref.at[pl.ds(dest_ref[0, 0, TOP_K * t + k], 1), :], sem)

    def issue(t, c):
        for k in range(TOP_K):
            row_copy(t, k).start(priority=k)
        return c

    def drain(t, c):
        for k in range(TOP_K):
            row_copy(t, k).wait()
        return c

    lax.fori_loop(0, tm, issue, 0, unroll=ROW_DMA_UNROLL)
    lax.fori_loop(0, tm, drain, 0, unroll=ROW_DMA_UNROLL)


def _dispatch(h, dest, rows):
    t = h.shape[0]
    tm = min(ROUTE_TILE, t)
    nt = t // tm
    xs0 = jnp.zeros((rows, D_MODEL), h.dtype)
    return pl.pallas_call(
        _dispatch_kernel,
        grid=(nt,),
        in_specs=[pl.BlockSpec((1, 1, TOP_K * tm), lambda i: (i, 0, 0), memory_space=pltpu.SMEM),
                  pl.BlockSpec((tm, D_MODEL), lambda i: (i, 0)),
                  pl.BlockSpec(memory_space=pl.ANY)],
        out_specs=pl.BlockSpec(memory_space=pl.ANY),
        out_shape=jax.ShapeDtypeStruct((rows, D_MODEL), h.dtype),
        scratch_shapes=[pltpu.SemaphoreType.DMA(())],
        input_output_aliases={2: 0},
        compiler_params=_cparams(("arbitrary",)),
        name="moe_dispatch",
    )(dest.reshape(nt, 1, TOP_K * tm), h, xs0)


def _experts_kernel(te_ref, nu_ref, x_ref, w1_ref, w3_ref, w2_ref, y_ref, xb_ref, act_ref):
    i = pl.program_id(0)
    j = pl.program_id(1)
    nf = act_ref.shape[0]
    used = i < nu_ref[0]

    @pl.when(jnp.logical_and(used, j == 0))
    def _():
        xb_ref[...] = x_ref[...].astype(BF16)

    @pl.when(jnp.logical_and(jnp.logical_not(used), j == 0))
    def _():
        y_ref[...] = jnp.zeros_like(y_ref)

    @pl.when(used)
    def _():
        xb = xb_ref[...]
        a = _mm(xb, w1_ref[0].astype(BF16))
        b = _mm(xb, w3_ref[0].astype(BF16))
        act_ref[j] = (jax.nn.silu(a) * b).astype(BF16)

    @pl.when(jnp.logical_and(used, j == nf - 1))
    def _():
        act = jnp.concatenate([act_ref[c] for c in range(nf)], axis=1)
        y_ref[...] = _mm(act, w2_ref[0])


def _experts(xs, tile_expert, n_used, w1, w3, w2):
    rows = xs.shape[0]
    tm = MOE_ROW_TILE
    f = w1.shape[2]
    tf = min(MOE_F_TILE, f)
    nf = f // tf

    def fchunk(i, j, nu):
        return jnp.where(i < nu[0], j, nf - 1)

    grid_spec = pltpu.PrefetchScalarGridSpec(
        num_scalar_prefetch=2,
        grid=(rows // tm, nf),
        in_specs=[pl.BlockSpec((tm, D_MODEL), lambda i, j, te, nu: (i, 0)),
                  pl.BlockSpec((1, D_MODEL, tf), lambda i, j, te, nu: (te[i], 0, fchunk(i, j, nu))),
                  pl.BlockSpec((1, D_MODEL, tf), lambda i, j, te, nu: (te[i], 0, fchunk(i, j, nu))),
                  pl.BlockSpec((1, f, D_MODEL), lambda i, j, te, nu: (te[i], 0, 0),
                               pipeline_mode=pl.Buffered(1))],
        out_specs=pl.BlockSpec((tm, D_MODEL), lambda i, j, te, nu: (i, 0)),
        scratch_shapes=[pltpu.VMEM((tm, D_MODEL), BF16),
                        pltpu.VMEM((nf, tm, tf), BF16)],
    )
    return pl.pallas_call(
        _experts_kernel,
        grid_spec=grid_spec,
        out_shape=jax.ShapeDtypeStruct((rows, D_MODEL), F32),
        compiler_params=_cparams(("arbitrary", "arbitrary")),
        name="moe_experts",
    )(tile_expert, n_used, xs, w1, w3, w2)


def _combine_kernel(dest_ref, x_ref, route_ref, nfin_ref, ys_ref, o_ref, buf_ref, sem):
    tm = x_ref.shape[0]

    def row_copy(t, k):
        return pltpu.make_async_copy(ys_ref.at[pl.ds(dest_ref[0, 0, TOP_K * t + k], 1), :],
                                     buf_ref.at[pl.ds(k * tm + t, 1), :], sem)

    def issue(t, c):
        for k in range(TOP_K):
            row_copy(t, k).start(priority=k)
        return c

    def drain(t, c):
        for k in range(TOP_K):
            row_copy(t, k).wait()
        return c

    lax.fori_loop(0, tm, issue, 0, unroll=ROW_DMA_UNROLL)
    lax.fori_loop(0, tm, drain, 0, unroll=ROW_DMA_UNROLL)
    route = route_ref[...]
    y = x_ref[...] + route[:, 2:3] * buf_ref[:tm, :] + route[:, 3:4] * buf_ref[tm:, :]
    o_ref[...] = _rms(y, nfin_ref[...])


def _combine(x, route, dest, ys, norm_final):
    t = x.shape[0]
    tm = min(ROUTE_TILE, t)
    nt = t // tm
    return pl.pallas_call(
        _combine_kernel,
        grid=(nt,),
        in_specs=[pl.BlockSpec((1, 1, TOP_K * tm), lambda i: (i, 0, 0), memory_space=pltpu.SMEM),
                  pl.BlockSpec((tm, D_MODEL), lambda i: (i, 0)),
                  pl.BlockSpec((tm, LANES), lambda i: (i, 0)),
                  _const_spec((1, D_MODEL)),
                  pl.BlockSpec(memory_space=pl.ANY)],
        out_specs=pl.BlockSpec((tm, D_MODEL), lambda i: (i, 0)),
        out_shape=jax.ShapeDtypeStruct((t, D_MODEL), F32),
        scratch_shapes=[pltpu.VMEM((TOP_K * tm, D_MODEL), F32), pltpu.SemaphoreType.DMA(())],
        compiler_params=_cparams(("arbitrary",)),
        name="moe_combine",
    )(dest.reshape(nt, 1, TOP_K * tm), x, route, norm_final.reshape(1, D_MODEL), ys)


def _moe_plan(route, row_tile):
    t = route.shape[0]
    m = t * TOP_K
    e_flat = route[:, :TOP_K].astype(jnp.int32).reshape(m)
    onehot = (e_flat[:, None] == jnp.arange(N_EXPERTS, dtype=jnp.int32)[None, :]).astype(jnp.int32)
    csum = jnp.cumsum(onehot, axis=0)
    rank = jnp.sum((csum - onehot) * onehot, axis=1)
    counts = csum[-1]
    padded = ((counts + row_tile - 1) // row_tile) * row_tile
    pad_ends = jnp.cumsum(padded)
    pad_starts = pad_ends - padded
    dest = jnp.sum(onehot * pad_starts[None, :], axis=1) + rank
    n_tiles = -(-m // row_tile) + N_EXPERTS
    tile_start = jnp.arange(n_tiles, dtype=jnp.int32) * row_tile
    tile_expert = jnp.minimum(jnp.sum((pad_ends[None, :] <= tile_start[:, None]).astype(jnp.int32), axis=1),
                              N_EXPERTS - 1)
    n_used = (pad_ends[-1] // row_tile).reshape(1)
    return dest.astype(jnp.int32), tile_expert.astype(jnp.int32), n_used.astype(jnp.int32), n_tiles * row_tile


def _prep_mixer_weights(l, w_in, q_norm, w_uq, kv_norm, w_ukv, w_o_attn, ssm_w_glu, ssm_b_glu, w_o_ssm,
                        conv_w, conv_b, conv_ln_g, conv_ln_b, w_o_conv, w_out, norm_ffn):
    o = IN_OFFSETS
    wi = w_in[l]
    low = wi[:, o[0]:o[3]]
    wa = jnp.pad(low, ((0, 0), (0, 4 * LANES - low.shape[1]))).astype(BF16)
    wq = jnp.pad(w_uq[l].reshape(Q_LORA, N_HEADS, QK_HEAD),
                 ((0, 0), (0, 0), (0, HEAD_PAD - QK_HEAD))).reshape(Q_LORA, N_HEADS * HEAD_PAD).astype(BF16)
    wkv = w_ukv[l].reshape(KV_LORA, N_HEADS, QK_NOPE + V_HEAD)
    wkn = jnp.pad(wkv[..., :QK_NOPE], ((0, 0), (0, 0), (0, HEAD_PAD - QK_NOPE))
                  ).reshape(KV_LORA, N_HEADS * HEAD_PAD).astype(BF16)
    wvt = wkv[..., QK_NOPE:].reshape(KV_LORA, N_HEADS * V_HEAD).T.astype(BF16)
    place = np.zeros((LANES, N_HEADS * HEAD_PAD), np.float32)
    for h in range(N_HEADS):
        for r in range(QK_ROPE):
            place[r, h * HEAD_PAD + QK_NOPE + r] = 1.0
    return dict(
        wa=wa, wut=wi[:, o[3]:o[4]].T.astype(BF16), wca=wi[:, o[4]:o[5]].astype(BF16),
        wcg=wi[:, o[5]:o[6]].astype(BF16), wg=wi[:, o[6]:o[7]].astype(BF16),
        qn=q_norm[l].reshape(1, Q_LORA), kvn=kv_norm[l].reshape(1, KV_LORA),
        wq=wq, wkn=wkn, wke=jnp.asarray(place, BF16), wvt=wvt,
        woa=w_o_attn[l].astype(BF16), wglut=ssm_w_glu[l].T.astype(BF16),
        bglu=ssm_b_glu[l].reshape(SSM_WIDTH, 1), wos=w_o_ssm[l].astype(BF16),
        cw=conv_w[l], cb=conv_b[l].reshape(1, CONV_WIDTH), lng=conv_ln_g[l].reshape(1, CONV_WIDTH),
        lnb=conv_ln_b[l].reshape(1, CONV_WIDTH), woc=w_o_conv[l].astype(BF16),
        wout=w_out[l].astype(BF16), nf=norm_ffn[l].reshape(1, D_MODEL))


def kernel(x, positions, norm_mix, w_in, q_norm, w_uq, kv_norm, w_ukv, w_o_attn, ssm_lam_re, ssm_lam_im, ssm_log_dt, ssm_b_re, ssm_b_im, ssm_c_re, ssm_c_im, ssm_d, ssm_w_glu, ssm_b_glu, w_o_ssm, conv_w, conv_b, conv_ln_g, conv_ln_b, w_o_conv, w_out, norm_ffn, ffn_w1, ffn_w3, ffn_w2, moe_router, moe_w1, moe_w3, moe_w2, norm_final):
    batch, seq, d = x.shape
    depth = w_in.shape[0]
    assert d == D_MODEL and depth == 2 and seq % SSM_CHUNK == 0
    t = batch * seq
    xf = x.reshape(t, d)
    cos_t, sin_t = _rope_tables(positions)
    ng = depth * SSM_GROUPS
    klag, win, wout_s, apr, api = _ssm_tables(
        ssm_lam_re.reshape(ng, SSM_STATE), ssm_lam_im.reshape(ng, SSM_STATE), ssm_log_dt.reshape(ng),
        ssm_b_re.reshape(ng, SSM_STATE, SSM_GROUP), ssm_b_im.reshape(ng, SSM_STATE, SSM_GROUP),
        ssm_c_re.reshape(ng, SSM_GROUP, SSM_STATE), ssm_c_im.reshape(ng, SSM_GROUP, SSM_STATE))

    for layer in range(depth):
        moe_layer = layer % 2 == 1
        w = _prep_mixer_weights(layer, w_in, q_norm, w_uq, kv_norm, w_ukv, w_o_attn, ssm_w_glu, ssm_b_glu,
                                w_o_ssm, conv_w, conv_b, conv_ln_g, conv_ln_b, w_o_conv, w_out, norm_ffn)
        q, k, vt, ut, zc, gates = _mixer_in(xf, norm_mix[layer].reshape(1, d), w, cos_t, sin_t)
        yt = _attention(q, k, vt, batch, seq)
        st = _ssm(ut, klag, win, wout_s, apr, api, ssm_d[layer], layer, batch)
        i = layer // 2
        if not moe_layer:
            xn, h2 = _merge(yt, st, zc, gates, xf, w, seq, with_router=False)
            xf = _ffn(xn, h2, ffn_w1[i].astype(BF16), ffn_w3[i].astype(BF16), ffn_w2[i].astype(BF16))
        else:
            wr = jnp.pad(moe_router[i], ((0, 0), (0, LANES - N_EXPERTS)))
            w["wrh"] = wr.astype(BF16)
            w["wrl"] = (wr - w["wrh"].astype(F32)).astype(BF16)
            xn, h2, route = _merge(yt, st, zc, gates, xf, w, seq, with_router=True)
            dest, tile_expert, n_used, rows = _moe_plan(route, MOE_ROW_TILE)
            xs = _dispatch(h2, dest, rows)
            ys = _experts(xs, tile_expert, n_used, moe_w1[i], moe_w3[i], moe_w2[i].astype(BF16))
            xf = _combine(xn, route, dest, ys, norm_final)
    return xf.reshape(batch, seq, d)
```

```python
import functools

import numpy as np
import jax
import jax.numpy as jnp
from jax import lax
from jax.experimental import pallas as pl
from jax.experimental.pallas import tpu as pltpu

F32 = jnp.float32
BF16 = jnp.bfloat16

D_MODEL = 1024
N_HEADS = 8
Q_LORA = 256
KV_LORA = 128
QK_NOPE = 64
QK_ROPE = 32
QK_HEAD = QK_NOPE + QK_ROPE
V_HEAD = 64
ROPE_THETA = 10000.0
SSM_WIDTH = 512
SSM_GROUP = 16
SSM_GROUPS = SSM_WIDTH // SSM_GROUP
SSM_STATE = 64
SSM_CHUNK = 128
CONV_WIDTH = 512
CONV_K = 31
N_EXPERTS = 8
TOP_K = 2
RMS_EPS = 1e-6
LN_EPS = 1e-5
IN_SPLITS = (Q_LORA, KV_LORA, QK_ROPE, SSM_WIDTH, CONV_WIDTH, CONV_WIDTH, 3 * D_MODEL)
IN_OFFSETS = tuple(int(v) for v in np.cumsum((0,) + IN_SPLITS))

LANES = 128
SUBLANES = 8
HEAD_PAD = LANES
VMEM_LIMIT_BYTES = 56 * 1024 * 1024

TOKEN_TILE = 512
ATTN_TILE = 1024
ATTN_COLS = 256
ATTN_ROWS = 256
ATTN_UNROLL = 4
MOE_ROW_TILE = 1024
MOE_F_TILE = 512
FFN_F_TILE = 512
ROUTE_TILE = 256
ROW_DMA_UNROLL = 8
CONV_HALO = 32
SUM_ROWS = 16
NEG_BIG = -1e30
LOG2_E = float(np.log2(np.e))


def _cparams(sem):
    return pltpu.CompilerParams(dimension_semantics=sem, vmem_limit_bytes=VMEM_LIMIT_BYTES)


def _const_spec(shape):
    nd = len(shape)
    return pl.BlockSpec(shape, lambda *_: (0,) * nd, pipeline_mode=pl.Buffered(1))


def _rms(x, g):
    return x * lax.rsqrt(jnp.mean(x * x, axis=-1, keepdims=True) + RMS_EPS) * g


def _mm(a, b):
    return jnp.dot(a, b, preferred_element_type=F32)


def _mm_nt(a, b):
    return lax.dot_general(a, b, (((1,), (1,)), ((), ())), preferred_element_type=F32)


def _mm_tn(a, b):
    return lax.dot_general(a, b, (((0,), (0,)), ((), ())), preferred_element_type=F32)


def _rope_tables_kernel(pos_ref, invf_ref, sign_ref, cos_ref, sin_ref):
    ang = pos_ref[...].astype(F32) * invf_ref[...]
    cos_ref[...] = jnp.cos(ang)
    sin_ref[...] = jnp.sin(ang) * sign_ref[...]


def _rope_tables(positions):
    t = positions.size
    tm = min(2048, t)
    half = QK_ROPE // 2
    inv_freq = ROPE_THETA ** (-np.arange(0, QK_ROPE, 2, dtype=np.float32) / QK_ROPE)
    invf = np.zeros((1, LANES), np.float32)
    sign = np.zeros((1, LANES), np.float32)
    invf[0, QK_NOPE:QK_NOPE + half] = inv_freq
    invf[0, QK_NOPE + half:QK_HEAD] = inv_freq
    sign[0, QK_NOPE:QK_NOPE + half] = -1.0
    sign[0, QK_NOPE + half:QK_HEAD] = 1.0
    return pl.pallas_call(
        _rope_tables_kernel,
        grid=(t // tm,),
        in_specs=[pl.BlockSpec((tm, 1), lambda i: (i, 0)),
                  pl.BlockSpec((1, LANES), lambda i: (0, 0)),
                  pl.BlockSpec((1, LANES), lambda i: (0, 0))],
        out_specs=[pl.BlockSpec((tm, LANES), lambda i: (i, 0))] * 2,
        out_shape=[jax.ShapeDtypeStruct((t, LANES), F32)] * 2,
        compiler_params=_cparams(("arbitrary",)),
        name="rope_tables",
    )(positions.reshape(t, 1), jnp.asarray(invf), jnp.asarray(sign))


def _rope_head(xh, cos, sin, lane):
    half = QK_ROPE // 2
    swapped = jnp.where(lane < QK_NOPE + half,
                        pltpu.roll(xh, LANES - half, 1), pltpu.roll(xh, half, 1))
    return xh * cos + swapped * sin


def _mixer_in_kernel(x_ref, nm_ref, wa_ref, wut_ref, wca_ref, wcg_ref, wg_ref, qn_ref, kvn_ref,
                     wq_ref, wkn_ref, wke_ref, wvt_ref, cos_ref, sin_ref,
                     q_ref, k_ref, vt_ref, ut_ref, zc_ref, g_ref):
    tm = x_ref.shape[0]
    hn = _rms(x_ref[...], nm_ref[...]).astype(BF16)
    za = _mm(hn, wa_ref[...])
    cqn = _rms(za[:, :Q_LORA], qn_ref[...]).astype(BF16)
    ckvn = _rms(za[:, Q_LORA:Q_LORA + KV_LORA], kvn_ref[...]).astype(BF16)
    kpe = za[:, Q_LORA + KV_LORA:].astype(BF16)
    q = _mm(cqn, wq_ref[...])
    k = _mm(ckvn, wkn_ref[...]) + _mm(kpe, wke_ref[...])
    cos = cos_ref[...]
    sin = sin_ref[...]
    lane = lax.broadcasted_iota(jnp.int32, (tm, LANES), 1)
    scale = QK_HEAD ** -0.5 * LOG2_E
    for h in range(N_HEADS):
        sl = slice(h * HEAD_PAD, (h + 1) * HEAD_PAD)
        q_ref[:, sl] = (_rope_head(q[:, sl], cos, sin, lane) * scale).astype(BF16)
        k_ref[:, sl] = _rope_head(k[:, sl], cos, sin, lane).astype(BF16)
    vt = _mm_nt(wvt_ref[...], ckvn)
    ut = _mm_nt(wut_ref[...], hn)
    for c in range(tm // LANES):
        sl = slice(c * LANES, (c + 1) * LANES)
        vt_ref[c] = vt[:, sl].astype(BF16)
        ut_ref[c] = ut[:, sl]
    zc_ref[...] = _mm(hn, wca_ref[...]) * jax.nn.sigmoid(_mm(hn, wcg_ref[...]))
    for c in range(3):
        sl = slice(c * D_MODEL, (c + 1) * D_MODEL)
        g_ref[:, sl] = jax.nn.sigmoid(_mm(hn, wg_ref[:, sl])).astype(BF16)


def _mixer_in(x, nm, w, cos_t, sin_t):
    t = x.shape[0]
    tm = min(TOKEN_TILE, t)
    nch = tm // LANES
    row = lambda c: pl.BlockSpec((tm, c), lambda i: (i, 0))
    chunked = lambda c: pl.BlockSpec((nch, c, LANES), lambda i: (i, 0, 0))
    weights = [nm, w["wa"], w["wut"], w["wca"], w["wcg"], w["wg"], w["qn"], w["kvn"],
               w["wq"], w["wkn"], w["wke"], w["wvt"]]
    return pl.pallas_call(
        _mixer_in_kernel,
        grid=(t // tm,),
        in_specs=[row(D_MODEL)] + [_const_spec(a.shape) for a in weights] + [row(LANES), row(LANES)],
        out_specs=[row(N_HEADS * HEAD_PAD), row(N_HEADS * HEAD_PAD), chunked(N_HEADS * V_HEAD),
                   chunked(SSM_WIDTH), row(CONV_WIDTH), row(3 * D_MODEL)],
        out_shape=[jax.ShapeDtypeStruct((t, N_HEADS * HEAD_PAD), BF16),
                   jax.ShapeDtypeStruct((t, N_HEADS * HEAD_PAD), BF16),
                   jax.ShapeDtypeStruct((t // LANES, N_HEADS * V_HEAD, LANES), BF16),
                   jax.ShapeDtypeStruct((t // LANES, SSM_WIDTH, LANES), F32),
                   jax.ShapeDtypeStruct((t, CONV_WIDTH), F32),
                   jax.ShapeDtypeStruct((t, 3 * D_MODEL), BF16)],
        compiler_params=_cparams(("arbitrary",)),
        name="mixer_in",
    )(x, *weights, cos_t, sin_t)


def _attn_kernel(q_ref, k_ref, vt_ref, o_ref, sa_ref, sb_ref, mxa_ref, mxb_ref, m_ref, acc_ref, *, tq, tk):
    qi = pl.program_id(2)
    cw = ATTN_COLS
    ncol = tq // cw
    nkc = tk // LANES
    rw = min(ATTN_ROWS, tk)
    ones_rows = (lax.broadcasted_iota(jnp.int32, (SUM_ROWS, rw), 0) == 0).astype(BF16)
    ping = (sa_ref, mxa_ref)
    pong = (sb_ref, mxb_ref)

    def score_pass(j, c, dst, masked):
        s_ref, mx_ref = dst
        qc = q_ref[pl.ds(pl.multiple_of(c * cw, cw), cw), :]
        mx = None
        for r in range(tk // rw):
            if masked and r * rw >= (c + 1) * cw:
                continue
            kb = k_ref[pl.ds(pl.multiple_of(j * tk + r * rw, rw), rw), :]
            st = _mm_nt(kb, qc)
            if masked and (r + 1) * rw > c * cw + 1:
                kpos = r * rw + lax.broadcasted_iota(jnp.int32, (rw, cw), 0)
                qpos = c * cw + lax.broadcasted_iota(jnp.int32, (rw, cw), 1)
                st = jnp.where(kpos <= qpos, st, NEG_BIG)
            s_ref[c, r * rw:(r + 1) * rw, :] = st
            mr = jnp.max(st, axis=0, keepdims=True)
            mx = mr if mx is None else jnp.maximum(mx, mr)
        mx_ref[c] = mx

    def value_pass(j, c, src, diag=False):
        s_ref, mx_ref = src
        m_old = m_ref[c]
        m_new = jnp.maximum(m_old, mx_ref[c])
        pv = None
        for r in range(tk // rw):
            if diag and r * rw >= (c + 1) * cw:
                continue
            p = jnp.exp2(s_ref[c, r * rw:(r + 1) * rw, :] - m_new).astype(BF16)
            vb = jnp.concatenate([vt_ref[j * nkc + r * (rw // LANES) + i] for i in range(rw // LANES)], axis=1)
            vb = jnp.concatenate([vb, ones_rows], axis=0)
            part = _mm(vb, p)
            pv = part if pv is None else pv + part
        acc_ref[c] = jnp.exp2(m_old - m_new) * acc_ref[c] + pv
        m_ref[c] = m_new

    def over_cols(fn, static=False):
        if static:
            for c in range(ncol):
                fn(c)
            return

        def body(c, carry):
            fn(c)
            return carry
        lax.fori_loop(0, ncol, body, 0, unroll=ATTN_UNROLL)

    def step(j, src, dst, masked):
        def fn(c):
            value_pass(j, c, src)
            score_pass(j + 1, c, dst, masked)
        over_cols(fn, static=masked)

    m_ref[...] = jnp.full(m_ref.shape, NEG_BIG, F32)
    acc_ref[...] = jnp.zeros(acc_ref.shape, F32)

    @pl.when(qi == 0)
    def _():
        over_cols(lambda c: score_pass(0, c, ping, True), static=True)

    @pl.when(qi > 0)
    def _():
        over_cols(lambda c: score_pass(0, c, ping, False))

    n_plain = jnp.maximum(qi - 1, 0)

    def pair(i, carry):
        step(2 * i, ping, pong, False)
        step(2 * i + 1, pong, ping, False)
        return carry

    lax.fori_loop(0, n_plain // 2, pair, 0)

    @pl.when(n_plain % 2 == 1)
    def _():
        step(n_plain - 1, ping, pong, False)

    @pl.when(jnp.logical_and(qi > 0, qi % 2 == 1))
    def _():
        step(qi - 1, ping, pong, True)
        over_cols(lambda c: value_pass(qi, c, pong, diag=True), static=True)

    @pl.when(jnp.logical_and(qi > 0, qi % 2 == 0))
    def _():
        step(qi - 1, pong, ping, True)

    @pl.when(qi % 2 == 0)
    def _():
        over_cols(lambda c: value_pass(qi, c, ping, diag=True), static=True)

    for c in range(ncol):
        acc = acc_ref[c]
        out = acc[:V_HEAD] / acc[V_HEAD:V_HEAD + 1]
        for i in range(cw // LANES):
            o_ref[c * (cw // LANES) + i] = out[:, i * LANES:(i + 1) * LANES].astype(BF16)


def _attention(q, k, vt, batch, seq):
    tq = tk = min(ATTN_TILE, seq)
    nq = seq // tq
    ncol = tq // ATTN_COLS
    kern = functools.partial(_attn_kernel, tq=tq, tk=tk)
    return pl.pallas_call(
        kern,
        grid=(batch, N_HEADS, nq),
        in_specs=[pl.BlockSpec((tq, HEAD_PAD), lambda b, h, i: (b * nq + i, h)),
                  pl.BlockSpec((seq, HEAD_PAD), lambda b, h, i: (b, h)),
                  pl.BlockSpec((seq // LANES, V_HEAD, LANES), lambda b, h, i: (b, h, 0))],
        out_specs=pl.BlockSpec((tq // LANES, V_HEAD, LANES), lambda b, h, i: (b * nq + i, h, 0)),
        out_shape=jax.ShapeDtypeStruct((batch * seq // LANES, N_HEADS * V_HEAD, LANES), BF16),
        scratch_shapes=[pltpu.VMEM((ncol, tk, ATTN_COLS), F32),
                        pltpu.VMEM((ncol, tk, ATTN_COLS), F32),
                        pltpu.VMEM((ncol, 1, ATTN_COLS), F32),
                        pltpu.VMEM((ncol, 1, ATTN_COLS), F32),
                        pltpu.VMEM((ncol, 1, ATTN_COLS), F32),
                        pltpu.VMEM((ncol, V_HEAD + SUM_ROWS, ATTN_COLS), F32)],
        compiler_params=_cparams(("arbitrary", "arbitrary", "arbitrary")),
        name="attention",
    )(q, k, vt)


def _ssm_tables_kernel(lrr_ref, lir_ref, lrc_ref, lic_ref, ldt_ref, btr_ref, bti_ref,
                       cr_ref, ci_ref, ctr_ref, cti_ref,
                       klag_ref, win_ref, wout_ref, apr_ref, api_ref):
    ch = SSM_CHUNK
    dt = jnp.exp(ldt_ref[0])

    def power(lr, li, n):
        mag = jnp.exp(lr * dt * n)
        return mag * jnp.cos(li * dt * n), mag * jnp.sin(li * dt * n)

    lr = jnp.minimum(lrr_ref[0], -1e-4)
    li = lir_ref[0]
    a_re, a_im = power(lr, li, 1.0)
    den = lr * lr + li * li
    nr, ni = a_re - 1.0, a_im
    coef_re = (nr * lr + ni * li) / den
    coef_im = (ni * lr - nr * li) / den
    btr, bti = btr_ref[0], bti_ref[0]
    bbr = coef_re * btr - coef_im * bti
    bbi = coef_re * bti + coef_im * btr
    cr, ci = cr_ref[0], ci_ref[0]
    f_re = jnp.concatenate([cr * bbr[i:i + 1] - ci * bbi[i:i + 1] for i in range(SSM_GROUP)], axis=0)
    f_im = jnp.concatenate([cr * bbi[i:i + 1] + ci * bbr[i:i + 1] for i in range(SSM_GROUP)], axis=0)

    lrc = jnp.minimum(lrc_ref[0], -1e-4)
    lic = lic_ref[0]
    lag = lax.broadcasted_iota(jnp.int32, (1, ch), 1).astype(F32)
    pl_re, pl_im = power(lrc, lic, lag)
    hi = lax.Precision.HIGHEST
    klag_ref[0] = (jnp.dot(f_re, pl_re, precision=hi, preferred_element_type=F32)
                   - jnp.dot(f_im, pl_im, precision=hi, preferred_element_type=F32))

    back = (ch - 1) - lax.broadcasted_iota(jnp.int32, (ch, 1), 0).astype(F32)
    q_re, q_im = power(lr, li, back)
    for i in range(SSM_GROUP):
        w_re = q_re * bbr[i:i + 1] - q_im * bbi[i:i + 1]
        w_im = q_re * bbi[i:i + 1] + q_im * bbr[i:i + 1]
        win_ref[0, i * ch:(i + 1) * ch, :] = jnp.concatenate([w_re, w_im], axis=1).astype(BF16)

    pw_re, pw_im = power(lrc, lic, lag + 1.0)
    ctr, cti = ctr_ref[0], cti_ref[0]
    p = SSM_STATE
    for o in range(SSM_GROUP):
        c_re, c_im = ctr[:, o:o + 1], cti[:, o:o + 1]
        wout_ref[0, :p, o * ch:(o + 1) * ch] = (c_re * pw_re - c_im * pw_im).astype(BF16)
        wout_ref[0, p:, o * ch:(o + 1) * ch] = (-(c_re * pw_im + c_im * pw_re)).astype(BF16)

    apr_ref[0], api_ref[0] = power(lr, li, float(ch))


def _ssm_tables(lam_re, lam_im, log_dt, b_re, b_im, c_re, c_im):
    n = lam_re.shape[0]
    p, g, ch = SSM_STATE, SSM_GROUP, SSM_CHUNK
    args = [lam_re.reshape(n, 1, p), lam_im.reshape(n, 1, p),
            lam_re.reshape(n, p, 1), lam_im.reshape(n, p, 1), log_dt.reshape(n, 1, 1),
            jnp.swapaxes(b_re, 1, 2), jnp.swapaxes(b_im, 1, 2), c_re, c_im,
            jnp.swapaxes(c_re, 1, 2), jnp.swapaxes(c_im, 1, 2)]
    spec = lambda a: pl.BlockSpec((1,) + a.shape[1:], lambda i: (i, 0, 0))
    out_shape = [jax.ShapeDtypeStruct((n, g * g, ch), F32),
                 jax.ShapeDtypeStruct((n, g * ch, 2 * p), BF16),
                 jax.ShapeDtypeStruct((n, 2 * p, g * ch), BF16),
                 jax.ShapeDtypeStruct((n, 1, p), F32),
                 jax.ShapeDtypeStruct((n, 1, p), F32)]
    return pl.pallas_call(
        _ssm_tables_kernel,
        grid=(n,),
        in_specs=[spec(a) for a in args],
        out_specs=[spec(s) for s in out_shape],
        out_shape=out_shape,
        compiler_params=_cparams(("arbitrary",)),
        name="ssm_tables",
    )(*args)


def _ssm_kernel(u_ref, klag_ref, win_ref, wout_ref, apr_ref, api_ref, d_ref, s_ref,
                mt_ref, hin_re, hin_im, hst_re, hst_im, *, batch):
    ch, g, p = SSM_CHUNK, SSM_GROUP, SSM_STATE
    nc = u_ref.shape[0]
    ncb = nc // batch
    causal = (lax.broadcasted_iota(jnp.int32, (ch, ch), 1)
              >= lax.broadcasted_iota(jnp.int32, (ch, ch), 0))

    def build(i, _):
        for o in range(g):
            kv = klag_ref[0, pl.ds(i * g + o, 1), :]
            tz = pltpu.roll(jnp.broadcast_to(kv, (ch, ch)), 0, 1, stride=1, stride_axis=0)
            mt_ref[pl.ds(pl.multiple_of(i * ch, ch), ch), o * ch:(o + 1) * ch] = (
                jnp.where(causal, tz, 0.0).astype(BF16))
        return 0

    lax.fori_loop(0, g, build, 0)

    u2 = u_ref.reshape(nc * g, LANES)
    s2 = s_ref.reshape(nc * g, LANES)
    lhs = jnp.concatenate([u2[pl.ds(i, nc, stride=g), :].astype(BF16) for i in range(g)], axis=1)
    y = _mm(lhs, mt_ref[...])
    hin = _mm(lhs, win_ref[0])
    hin_re[...] = hin[:, :p]
    hin_im[...] = hin[:, p:]
    a_re, a_im = apr_ref[0], api_ref[0]

    def scan(c, carry):
        new = []
        for b in range(batch):
            h_re, h_im = carry[b]
            r = b * ncb + c
            hst_re[pl.ds(r, 1), :] = h_re
            hst_im[pl.ds(r, 1), :] = h_im
            x_re = hin_re[pl.ds(r, 1), :]
            x_im = hin_im[pl.ds(r, 1), :]
            new.append((a_re * h_re - a_im * h_im + x_re, a_re * h_im + a_im * h_re + x_im))
        return tuple(new)

    zero = jnp.zeros((1, p), F32)
    lax.fori_loop(0, ncb, scan, tuple((zero, zero) for _ in range(batch)))
    y = y + _mm(hst_re[...].astype(BF16), wout_ref[0, :p, :]) + _mm(hst_im[...].astype(BF16), wout_ref[0, p:, :])
    for o in range(g):
        yo = y[:, o * ch:(o + 1) * ch] + u2[pl.ds(o, nc, stride=g), :] * d_ref[0, o:o + 1, :]
        s2[pl.ds(o, nc, stride=g), :] = jax.nn.gelu(yo)


def _ssm(ut, klag, win, wout, apr, api, d_skip, layer, batch):
    nc = ut.shape[0]
    g, ch, p = SSM_GROUP, SSM_CHUNK, SSM_STATE
    base = layer * SSM_GROUPS
    tab = lambda a: pl.BlockSpec((1,) + a.shape[1:], lambda i: (base + i, 0, 0))
    kern = functools.partial(_ssm_kernel, batch=batch)
    return pl.pallas_call(
        kern,
        grid=(SSM_GROUPS,),
        in_specs=[pl.BlockSpec((nc, g, LANES), lambda i: (0, i, 0)),
                  tab(klag), tab(win), tab(wout), tab(apr), tab(api),
                  pl.BlockSpec((1, g, 1), lambda i: (i, 0, 0))],
        out_specs=pl.BlockSpec((nc, g, LANES), lambda i: (0, i, 0)),
        out_shape=jax.ShapeDtypeStruct(ut.shape, F32),
        scratch_shapes=[pltpu.VMEM((g * ch, g * ch), BF16),
                        pltpu.VMEM((nc, p), F32), pltpu.VMEM((nc, p), F32),
                        pltpu.VMEM((nc, p), F32), pltpu.VMEM((nc, p), F32)],
        compiler_params=_cparams(("arbitrary",)),
        name="ssm",
    )(ut, klag, win, wout, apr, api, d_skip.reshape(SSM_GROUPS, g, 1))


def _merge_kernel(yt_ref, st_ref, zc_ref, halo_ref, g_ref, x_ref,
                  woa_ref, wglut_ref, bglu_ref, wos_ref, cw_ref, cb_ref, lng_ref, lnb_ref,
                  woc_ref, wout_ref, nf_ref, *rest, tiles_per_seq, with_router):
    if with_router:
        wrh_ref, wrl_ref, xn_ref, h2_ref, route_ref = rest
    else:
        xn_ref, h2_ref = rest
    tm = x_ref.shape[0]
    nch = tm // LANES
    i = pl.program_id(0)

    yt = jnp.concatenate([yt_ref[c] for c in range(nch)], axis=1)
    y_attn = _mm_tn(yt, woa_ref[...])

    st = jnp.concatenate([st_ref[c] for c in range(nch)], axis=1)
    gate = jax.nn.sigmoid(_mm(wglut_ref[...], st.astype(BF16)) + bglu_ref[...])
    y_ssm = _mm_tn((st * gate).astype(BF16), wos_ref[...])

    first = (i % tiles_per_seq) == 0
    zs = jnp.concatenate([jnp.where(first, 0.0, halo_ref[...]), zc_ref[...]], axis=0)
    nz = CONV_HALO + tm
    off = CONV_HALO - (CONV_K - 1)
    conv = cb_ref[...]
    for res in range(SUBLANES):
        zr = zs if res == 0 else pltpu.roll(zs, nz - res, 0)
        for kk in range(CONV_K):
            if (off + kk) % SUBLANES == res:
                lo = off + kk - res
                conv = conv + zr[lo:lo + tm, :] * cw_ref[kk:kk + 1, :]
    mu = jnp.mean(conv, axis=-1, keepdims=True)
    cen = conv - mu
    var = jnp.mean(cen * cen, axis=-1, keepdims=True)
    yc = jax.nn.silu(cen * lax.rsqrt(var + LN_EPS) * lng_ref[...] + lnb_ref[...])
    y_conv = _mm(yc.astype(BF16), woc_ref[...])

    d = D_MODEL
    merged = (g_ref[:, :d].astype(F32) * y_attn + g_ref[:, d:2 * d].astype(F32) * y_ssm
              + g_ref[:, 2 * d:].astype(F32) * y_conv)
    xn = x_ref[...] + _mm(merged.astype(BF16), wout_ref[...])
    xn_ref[...] = xn
    h = _rms(xn, nf_ref[...])
    h2_ref[...] = h.astype(h2_ref.dtype)

    if with_router:
        h_hi = h.astype(BF16)
        h_lo = (h - h_hi.astype(F32)).astype(BF16)
        logits = _mm(h_hi, wrh_ref[...]) + _mm(h_lo, wrh_ref[...]) + _mm(h_hi, wrl_ref[...])
        lane = lax.broadcasted_iota(jnp.int32, (tm, LANES), 1)
        logits = jnp.where(lane < N_EXPERTS, logits, NEG_BIG)
        m1 = jnp.max(logits, axis=-1, keepdims=True)
        i1 = jnp.min(jnp.where(logits == m1, lane, LANES), axis=-1, keepdims=True)
        rest_l = jnp.where(lane == i1, NEG_BIG, logits)
        m2 = jnp.max(rest_l, axis=-1, keepdims=True)
        i2 = jnp.min(jnp.where(rest_l == m2, lane, LANES), axis=-1, keepdims=True)
        e = jnp.exp(m2 - m1)
        g1 = 1.0 / (1.0 + e)
        g2 = e / (1.0 + e)
        route_ref[...] = jnp.where(lane == 0, i1.astype(F32),
                         jnp.where(lane == 1, i2.astype(F32),
                         jnp.where(lane == 2, g1, jnp.where(lane == 3, g2, 0.0))))


def _merge(yt, st, zc, gates, x, w, seq, with_router):
    t = x.shape[0]
    tm = min(TOKEN_TILE, t, seq)
    nch = tm // LANES
    hpt = tm // CONV_HALO
    row = lambda c: pl.BlockSpec((tm, c), lambda i: (i, 0))
    chunked = lambda c: pl.BlockSpec((nch, c, LANES), lambda i: (i, 0, 0))
    weights = [w["woa"], w["wglut"], w["bglu"], w["wos"], w["cw"], w["cb"], w["lng"], w["lnb"],
               w["woc"], w["wout"], w["nf"]]
    out_shape = [jax.ShapeDtypeStruct((t, D_MODEL), F32),
                 jax.ShapeDtypeStruct((t, D_MODEL), F32 if with_router else BF16)]
    out_specs = [row(D_MODEL), row(D_MODEL)]
    if with_router:
        weights += [w["wrh"], w["wrl"]]
        out_shape.append(jax.ShapeDtypeStruct((t, LANES), F32))
        out_specs.append(row(LANES))
    kern = functools.partial(_merge_kernel, tiles_per_seq=seq // tm, with_router=with_router)
    return pl.pallas_call(
        kern,
        grid=(t // tm,),
        in_specs=[chunked(N_HEADS * V_HEAD), chunked(SSM_WIDTH), row(CONV_WIDTH),
                  pl.BlockSpec((CONV_HALO, CONV_WIDTH), lambda i: (jnp.maximum(i * hpt - 1, 0), 0)),
                  row(3 * D_MODEL), row(D_MODEL)] + [_const_spec(a.shape) for a in weights],
        out_specs=out_specs,
        out_shape=out_shape,
        compiler_params=_cparams(("arbitrary",)),
        name="merge_router" if with_router else "merge",
    )(yt, st, zc, zc, gates, x, *weights)


def _ffn_kernel(x_ref, h_ref, w1_ref, w3_ref, w2_ref, o_ref):
    h = h_ref[...]
    f = w1_ref.shape[1]
    acc = x_ref[...]
    for lo in range(0, f, FFN_F_TILE):
        hi = min(lo + FFN_F_TILE, f)
        a = _mm(h, w1_ref[:, lo:hi])
        b = _mm(h, w3_ref[:, lo:hi])
        acc = acc + _mm((jax.nn.silu(a) * b).astype(BF16), w2_ref[lo:hi, :])
    o_ref[...] = acc


def _ffn(x, h, w1, w3, w2):
    t = x.shape[0]
    tm = min(TOKEN_TILE, t)
    row = pl.BlockSpec((tm, D_MODEL), lambda i: (i, 0))
    return pl.pallas_call(
        _ffn_kernel,
        grid=(t // tm,),
        in_specs=[row, row, _const_spec(w1.shape), _const_spec(w3.shape), _const_spec(w2.shape)],
        out_specs=row,
        out_shape=jax.ShapeDtypeStruct((t, D_MODEL), F32),
        compiler_params=_cparams(("arbitrary",)),
        name="ffn",
    )(x, h, w1, w3, w2)


def _dispatch_kernel(dest_ref, h_ref, xs_in_ref, xs_ref, sem):
    del xs_in_ref
    tm = h_ref.shape[0]

    def row_copy(t, k):
        return pltpu.make_async_copy(h_ref.at[pl.ds(t, 1), :],
                                     xs_ref.at[pl.ds(dest_ref[0, 0, TOP_K * t + k], 1), :], sem)

    def issue(t, c):
        for k in range(TOP_K):
            row_copy(t, k).start(priority=k)
        return c

    def drain(t, c):
        for k in range(TOP_K):
            row_copy(t, k).wait()
        return c

    lax.fori_loop(0, tm, issue, 0, unroll=ROW_DMA_UNROLL)
    lax.fori_loop(0, tm, drain, 0, unroll=ROW_DMA_UNROLL)


def _dispatch(h, dest, rows):
    t = h.shape[0]
    tm = min(ROUTE_TILE, t)
    nt = t // tm
    xs0 = jnp.zeros((rows, D_MODEL), h.dtype)
    return pl.pallas_call(
        _dispatch_kernel,
        grid=(nt,),
        in_specs=[pl.BlockSpec((1, 1, TOP_K * tm), lambda i: (i, 0, 0), memory_space=pltpu.SMEM),
                  pl.BlockSpec((tm, D_MODEL), lambda i: (i, 0)),
                  pl.BlockSpec(memory_space=pl.ANY)],
        out_specs=pl.BlockSpec(memory_space=pl.ANY),
        out_shape=jax.ShapeDtypeStruct((rows, D_MODEL), h.dtype),
        scratch_shapes=[pltpu.SemaphoreType.DMA(())],
        input_output_aliases={2: 0},
        compiler_params=_cparams(("arbitrary",)),
        name="moe_dispatch",
    )(dest.reshape(nt, 1, TOP_K * tm), h, xs0)


def _experts_kernel(te_ref, nu_ref, x_ref, w1_ref, w3_ref, w2_ref, y_ref, xb_ref, act_ref):
    i = pl.program_id(0)
    j = pl.program_id(1)
    nf = act_ref.shape[0]
    used = i < nu_ref[0]

    @pl.when(jnp.logical_and(used, j == 0))
    def _():
        xb_ref[...] = x_ref[...].astype(BF16)

    @pl.when(jnp.logical_and(jnp.logical_not(used), j == 0))
    def _():
        y_ref[...] = jnp.zeros_like(y_ref)

    @pl.when(used)
    def _():
        xb = xb_ref[...]
        a = _mm(xb, w1_ref[0])
        b = _mm(xb, w3_ref[0])
        act_ref[j] = (jax.nn.silu(a) * b).astype(BF16)

    @pl.when(jnp.logical_and(used, j == nf - 1))
    def _():
        act = jnp.concatenate([act_ref[c] for c in range(nf)], axis=1)
        y_ref[...] = _mm(act, w2_ref[0])


def _experts(xs, tile_expert, n_used, w1, w3, w2):
    rows = xs.shape[0]
    tm = MOE_ROW_TILE
    f = w1.shape[2]
    tf = min(MOE_F_TILE, f)
    nf = f // tf

    def fchunk(i, j, nu):
        return jnp.where(i < nu[0], j, nf - 1)

    grid_spec = pltpu.PrefetchScalarGridSpec(
        num_scalar_prefetch=2,
        grid=(rows // tm, nf),
        in_specs=[pl.BlockSpec((tm, D_MODEL), lambda i, j, te, nu: (i, 0)),
                  pl.BlockSpec((1, D_MODEL, tf), lambda i, j, te, nu: (te[i], 0, fchunk(i, j, nu))),
                  pl.BlockSpec((1, D_MODEL, tf), lambda i, j, te, nu: (te[i], 0, fchunk(i, j, nu))),
                  pl.BlockSpec((1, f, D_MODEL), lambda i, j, te, nu: (te[i], 0, 0),
                               pipeline_mode=pl.Buffered(1))],
        out_specs=pl.BlockSpec((tm, D_MODEL), lambda i, j, te, nu: (i, 0)),
        scratch_shapes=[pltpu.VMEM((tm, D_MODEL), BF16),
                        pltpu.VMEM((nf, tm, tf), BF16)],
    )
    return pl.pallas_call(
        _experts_kernel,
        grid_spec=grid_spec,
        out_shape=jax.ShapeDtypeStruct((rows, D_MODEL), F32),
        compiler_params=_cparams(("arbitrary", "arbitrary")),
        name="moe_experts",
    )(tile_expert, n_used, xs, w1, w3, w2)


def _combine_kernel(dest_ref, x_ref, route_ref, nfin_ref, ys_ref, o_ref, buf_ref, sem):
    tm = x_ref.shape[0]

    def row_copy(t, k):
        return pltpu.make_async_copy(ys_ref.at[pl.ds(dest_ref[0, 0, TOP_K * t + k], 1), :],
                                     buf_ref.at[pl.ds(k * tm + t, 1), :], sem)

    def issue(t, c):
        for k in range(TOP_K):
            row_copy(t, k).start(priority=k)
        return c

    def drain(t, c):
        for k in range(TOP_K):
            row_copy(t, k).wait()
        return c

    lax.fori_loop(0, tm, issue, 0, unroll=ROW_DMA_UNROLL)
    lax.fori_loop(0, tm, drain, 0, unroll=ROW_DMA_UNROLL)
    route = route_ref[...]
    y = x_ref[...] + route[:, 2:3] * buf_ref[:tm, :] + route[:, 3:4] * buf_ref[tm:, :]
    o_ref[...] = _rms(y, nfin_ref[...])


def _combine(x, route, dest, ys, norm_final):
    t = x.shape[0]
    tm = min(ROUTE_TILE, t)
    nt = t // tm
    return pl.pallas_call(
        _combine_kernel,
        grid=(nt,),
        in_specs=[pl.BlockSpec((1, 1, TOP_K * tm), lambda i: (i, 0, 0), memory_space=pltpu.SMEM),
                  pl.BlockSpec((tm, D_MODEL), lambda i: (i, 0)),
                  pl.BlockSpec((tm, LANES), lambda i: (i, 0)),
                  _const_spec((1, D_MODEL)),
                  pl.BlockSpec(memory_space=pl.ANY)],
        out_specs=pl.BlockSpec((tm, D_MODEL), lambda i: (i, 0)),
        out_shape=jax.ShapeDtypeStruct((t, D_MODEL), F32),
        scratch_shapes=[pltpu.VMEM((TOP_K * tm, D_MODEL), F32), pltpu.SemaphoreType.DMA(())],
        compiler_params=_cparams(("arbitrary",)),
        name="moe_combine",
    )(dest.reshape(nt, 1, TOP_K * tm), x, route, norm_final.reshape(1, D_MODEL), ys)


def _moe_plan(route, row_tile):
    t = route.shape[0]
    m = t * TOP_K
    e_flat = route[:, :TOP_K].astype(jnp.int32).reshape(m)
    onehot = (e_flat[:, None] == jnp.arange(N_EXPERTS, dtype=jnp.int32)[None, :]).astype(jnp.int32)
    csum = jnp.cumsum(onehot, axis=0)
    rank = jnp.sum((csum - onehot) * onehot, axis=1)
    counts = csum[-1]
    padded = ((counts + row_tile - 1) // row_tile) * row_tile
    pad_ends = jnp.cumsum(padded)
    pad_starts = pad_ends - padded
    dest = jnp.sum(onehot * pad_starts[None, :], axis=1) + rank
    n_tiles = -(-m // row_tile) + N_EXPERTS
    tile_start = jnp.arange(n_tiles, dtype=jnp.int32) * row_tile
    tile_expert = jnp.minimum(jnp.sum((pad_ends[None, :] <= tile_start[:, None]).astype(jnp.int32), axis=1),
                              N_EXPERTS - 1)
    n_used = (pad_ends[-1] // row_tile).reshape(1)
    return dest.astype(jnp.int32), tile_expert.astype(jnp.int32), n_used.astype(jnp.int32), n_tiles * row_tile


def _prep_mixer_weights(l, w_in, q_norm, w_uq, kv_norm, w_ukv, w_o_attn, ssm_w_glu, ssm_b_glu, w_o_ssm,
                        conv_w, conv_b, conv_ln_g, conv_ln_b, w_o_conv, w_out, norm_ffn):
    o = IN_OFFSETS
    wi = w_in[l]
    low = wi[:, o[0]:o[3]]
    wa = jnp.pad(low, ((0, 0), (0, 4 * LANES - low.shape[1]))).astype(BF16)
    wq = jnp.pad(w_uq[l].reshape(Q_LORA, N_HEADS, QK_HEAD),
                 ((0, 0), (0, 0), (0, HEAD_PAD - QK_HEAD))).reshape(Q_LORA, N_HEADS * HEAD_PAD).astype(BF16)
    wkv = w_ukv[l].reshape(KV_LORA, N_HEADS, QK_NOPE + V_HEAD)
    wkn = jnp.pad(wkv[..., :QK_NOPE], ((0, 0), (0, 0), (0, HEAD_PAD - QK_NOPE))
                  ).reshape(KV_LORA, N_HEADS * HEAD_PAD).astype(BF16)
    wvt = wkv[..., QK_NOPE:].reshape(KV_LORA, N_HEADS * V_HEAD).T.astype(BF16)
    place = np.zeros((LANES, N_HEADS * HEAD_PAD), np.float32)
    for h in range(N_HEADS):
        for r in range(QK_ROPE):
            place[r, h * HEAD_PAD + QK_NOPE + r] = 1.0
    return dict(
        wa=wa, wut=wi[:, o[3]:o[4]].T.astype(BF16), wca=wi[:, o[4]:o[5]].astype(BF16),
        wcg=wi[:, o[5]:o[6]].astype(BF16), wg=wi[:, o[6]:o[7]].astype(BF16),
        qn=q_norm[l].reshape(1, Q_LORA), kvn=kv_norm[l].reshape(1, KV_LORA),
        wq=wq, wkn=wkn, wke=jnp.asarray(place, BF16), wvt=wvt,
        woa=w_o_attn[l].astype(BF16), wglut=ssm_w_glu[l].T.astype(BF16),
        bglu=ssm_b_glu[l].reshape(SSM_WIDTH, 1), wos=w_o_ssm[l].astype(BF16),
        cw=conv_w[l], cb=conv_b[l].reshape(1, CONV_WIDTH), lng=conv_ln_g[l].reshape(1, CONV_WIDTH),
        lnb=conv_ln_b[l].reshape(1, CONV_WIDTH), woc=w_o_conv[l].astype(BF16),
        wout=w_out[l].astype(BF16), nf=norm_ffn[l].reshape(1, D_MODEL))


def kernel(x, positions, norm_mix, w_in, q_norm, w_uq, kv_norm, w_ukv, w_o_attn, ssm_lam_re, ssm_lam_im, ssm_log_dt, ssm_b_re, ssm_b_im, ssm_c_re, ssm_c_im, ssm_d, ssm_w_glu, ssm_b_glu, w_o_ssm, conv_w, conv_b, conv_ln_g, conv_ln_b, w_o_conv, w_out, norm_ffn, ffn_w1, ffn_w3, ffn_w2, moe_router, moe_w1, moe_w3, moe_w2, norm_final):
    batch, seq, d = x.shape
    depth = w_in.shape[0]
    assert d == D_MODEL and depth == 2 and seq % SSM_CHUNK == 0
    t = batch * seq
    xf = x.reshape(t, d)
    cos_t, sin_t = _rope_tables(positions)
    ng = depth * SSM_GROUPS
    klag, win, wout_s, apr, api = _ssm_tables(
        ssm_lam_re.reshape(ng, SSM_STATE), ssm_lam_im.reshape(ng, SSM_STATE), ssm_log_dt.reshape(ng),
        ssm_b_re.reshape(ng, SSM_STATE, SSM_GROUP), ssm_b_im.reshape(ng, SSM_STATE, SSM_GROUP),
        ssm_c_re.reshape(ng, SSM_GROUP, SSM_STATE), ssm_c_im.reshape(ng, SSM_GROUP, SSM_STATE))

    for layer in range(depth):
        moe_layer = layer % 2 == 1
        w = _prep_mixer_weights(layer, w_in, q_norm, w_uq, kv_norm, w_ukv, w_o_attn, ssm_w_glu, ssm_b_glu,
                                w_o_ssm, conv_w, conv_b, conv_ln_g, conv_ln_b, w_o_conv, w_out, norm_ffn)
        q, k, vt, ut, zc, gates = _mixer_in(xf, norm_mix[layer].reshape(1, d), w, cos_t, sin_t)
        yt = _attention(q, k, vt, batch, seq)
        st = _ssm(ut, klag, win, wout_s, apr, api, ssm_d[layer], layer, batch)
        i = layer // 2
        if not moe_layer:
            xn, h2 = _merge(yt, st, zc, gates, xf, w, seq, with_router=False)
            xf = _ffn(xn, h2, ffn_w1[i].astype(BF16), ffn_w3[i].astype(BF16), ffn_w2[i].astype(BF16))
        else:
            wr = jnp.pad(moe_router[i], ((0, 0), (0, LANES - N_EXPERTS)))
            w["wrh"] = wr.astype(BF16)
            w["wrl"] = (wr - w["wrh"].astype(F32)).astype(BF16)
            xn, h2, route = _merge(yt, st, zc, gates, xf, w, seq, with_router=True)
            dest, tile_expert, n_used, rows = _moe_plan(route, MOE_ROW_TILE)
            xs = _dispatch(h2, dest, rows)
            ys = _experts(xs, tile_expert, n_used, moe_w1[i].astype(BF16), moe_w3[i].astype(BF16),
                          moe_w2[i].astype(BF16))
            xf = _combine(xn, route, dest, ys, norm_final)
    return xf.reshape(batch, seq, d)
```

```python
import functools

import numpy as np
import jax
import jax.numpy as jnp
from jax import lax
from jax.experimental import pallas as pl
from jax.experimental.pallas import tpu as pltpu

F32 = jnp.float32
BF16 = jnp.bfloat16

D_MODEL = 1024
N_HEADS = 8
Q_LORA = 256
KV_LORA = 128
QK_NOPE = 64
QK_ROPE = 32
QK_HEAD = QK_NOPE + QK_ROPE
V_HEAD = 64
ROPE_THETA = 10000.0
SSM_WIDTH = 512
SSM_GROUP = 16
SSM_GROUPS = SSM_WIDTH // SSM_GROUP
SSM_STATE = 64
SSM_CHUNK = 128
CONV_WIDTH = 512
CONV_K = 31
N_EXPERTS = 8
TOP_K = 2
RMS_EPS = 1e-6
LN_EPS = 1e-5
LAM_RE_MAX = -1e-4
IN_SPLITS = (Q_LORA, KV_LORA, QK_ROPE, SSM_WIDTH, CONV_WIDTH, CONV_WIDTH, 3 * D_MODEL)
IN_OFFSETS = tuple(int(v) for v in np.cumsum((0,) + IN_SPLITS))

LANES = 128
SUBLANES = 8
HEAD_PAD = LANES
VMEM_LIMIT_BYTES = 56 * 1024 * 1024

TOKEN_TILE = 512
ATTN_TILE = 1024
ATTN_COLS = 256
ATTN_ROWS = 256
ATTN_UNROLL = 4
MOE_ROW_TILE = 1024
MOE_F_TILE = 512
FFN_F_TILE = 512
ROUTE_TILE = 512
ROPE_TILE = 2048
ROW_DMA_UNROLL = 8
CONV_HALO = 32
SUM_ROWS = 16
NEG_BIG = -1e30
LOG2_E = float(np.log2(np.e))


def _cparams(sem):
    return pltpu.CompilerParams(dimension_semantics=sem, vmem_limit_bytes=VMEM_LIMIT_BYTES)


def _const_spec(shape):
    nd = len(shape)
    return pl.BlockSpec(shape, lambda *_: (0,) * nd, pipeline_mode=pl.Buffered(1))


def _rms(x, g):
    return x * lax.rsqrt(jnp.mean(x * x, axis=-1, keepdims=True) + RMS_EPS) * g


def _mm(a, b):
    return jnp.dot(a, b, preferred_element_type=F32)


def _mm_nt(a, b):
    return lax.dot_general(a, b, (((1,), (1,)), ((), ())), preferred_element_type=F32)


def _mm_tn(a, b):
    return lax.dot_general(a, b, (((0,), (0,)), ((), ())), preferred_element_type=F32)


def _rope_tables_kernel(pos_ref, invf_ref, sign_ref, cos_ref, sin_ref):
    ang = pos_ref[...].astype(F32) * invf_ref[...]
    cos_ref[...] = jnp.cos(ang)
    sin_ref[...] = jnp.sin(ang) * sign_ref[...]


def _rope_tables(positions):
    t = positions.size
    tm = min(ROPE_TILE, t)
    half = QK_ROPE // 2
    inv_freq = ROPE_THETA ** (-np.arange(0, QK_ROPE, 2, dtype=np.float32) / QK_ROPE)
    invf = np.zeros((1, LANES), np.float32)
    sign = np.zeros((1, LANES), np.float32)
    invf[0, QK_NOPE:QK_NOPE + half] = inv_freq
    invf[0, QK_NOPE + half:QK_HEAD] = inv_freq
    sign[0, QK_NOPE:QK_NOPE + half] = -1.0
    sign[0, QK_NOPE + half:QK_HEAD] = 1.0
    return pl.pallas_call(
        _rope_tables_kernel,
        grid=(t // tm,),
        in_specs=[pl.BlockSpec((tm, 1), lambda i: (i, 0)),
                  pl.BlockSpec((1, LANES), lambda i: (0, 0)),
                  pl.BlockSpec((1, LANES), lambda i: (0, 0))],
        out_specs=[pl.BlockSpec((tm, LANES), lambda i: (i, 0))] * 2,
        out_shape=[jax.ShapeDtypeStruct((t, LANES), F32)] * 2,
        compiler_params=_cparams(("arbitrary",)),
        name="rope_tables",
    )(positions.reshape(t, 1), jnp.asarray(invf), jnp.asarray(sign))


def _rope_head(xh, cos, sin, lane):
    half = QK_ROPE // 2
    swapped = jnp.where(lane < QK_NOPE + half,
                        pltpu.roll(xh, LANES - half, 1), pltpu.roll(xh, half, 1))
    return xh * cos + swapped * sin


def _mixer_in_kernel(x_ref, nm_ref, wa_ref, wut_ref, wca_ref, wcg_ref, wg_ref, qn_ref, kvn_ref,
                     wq_ref, wkn_ref, wke_ref, wvt_ref, cos_ref, sin_ref,
                     q_ref, k_ref, vt_ref, ut_ref, zc_ref, g_ref):
    tm = x_ref.shape[0]
    hn = _rms(x_ref[...], nm_ref[...]).astype(BF16)
    za = _mm(hn, wa_ref[...])
    cqn = _rms(za[:, :Q_LORA], qn_ref[...]).astype(BF16)
    ckvn = _rms(za[:, Q_LORA:Q_LORA + KV_LORA], kvn_ref[...]).astype(BF16)
    kpe = za[:, Q_LORA + KV_LORA:].astype(BF16)
    q = _mm(cqn, wq_ref[...])
    k = _mm(ckvn, wkn_ref[...]) + _mm(kpe, wke_ref[...])
    cos = cos_ref[...]
    sin = sin_ref[...]
    lane = lax.broadcasted_iota(jnp.int32, (tm, LANES), 1)
    scale = QK_HEAD ** -0.5 * LOG2_E
    for h in range(N_HEADS):
        sl = slice(h * HEAD_PAD, (h + 1) * HEAD_PAD)
        q_ref[:, sl] = (_rope_head(q[:, sl], cos, sin, lane) * scale).astype(BF16)
        k_ref[:, sl] = _rope_head(k[:, sl], cos, sin, lane).astype(BF16)
    vt = _mm_nt(wvt_ref[...], ckvn)
    ut = _mm_nt(wut_ref[...], hn)
    for c in range(tm // LANES):
        sl = slice(c * LANES, (c + 1) * LANES)
        vt_ref[c] = vt[:, sl].astype(BF16)
        ut_ref[c] = ut[:, sl]
    zc_ref[...] = _mm(hn, wca_ref[...]) * jax.nn.sigmoid(_mm(hn, wcg_ref[...]))
    for c in range(3):
        sl = slice(c * D_MODEL, (c + 1) * D_MODEL)
        g_ref[:, sl] = jax.nn.sigmoid(_mm(hn, wg_ref[:, sl])).astype(BF16)


def _mixer_in(x, nm, w, cos_t, sin_t):
    t = x.shape[0]
    tm = min(TOKEN_TILE, t)
    nch = tm // LANES
    row = lambda c: pl.BlockSpec((tm, c), lambda i: (i, 0))
    chunked = lambda c: pl.BlockSpec((nch, c, LANES), lambda i: (i, 0, 0))
    weights = [nm, w["wa"], w["wut"], w["wca"], w["wcg"], w["wg"], w["qn"], w["kvn"],
               w["wq"], w["wkn"], w["wke"], w["wvt"]]
    return pl.pallas_call(
        _mixer_in_kernel,
        grid=(t // tm,),
        in_specs=[row(D_MODEL)] + [_const_spec(a.shape) for a in weights] + [row(LANES), row(LANES)],
        out_specs=[row(N_HEADS * HEAD_PAD), row(N_HEADS * HEAD_PAD), chunked(N_HEADS * V_HEAD),
                   chunked(SSM_WIDTH), row(CONV_WIDTH), row(3 * D_MODEL)],
        out_shape=[jax.ShapeDtypeStruct((t, N_HEADS * HEAD_PAD), BF16),
                   jax.ShapeDtypeStruct((t, N_HEADS * HEAD_PAD), BF16),
                   jax.ShapeDtypeStruct((t // LANES, N_HEADS * V_HEAD, LANES), BF16),
                   jax.ShapeDtypeStruct((t // LANES, SSM_WIDTH, LANES), F32),
                   jax.ShapeDtypeStruct((t, CONV_WIDTH), F32),
                   jax.ShapeDtypeStruct((t, 3 * D_MODEL), BF16)],
        compiler_params=_cparams(("arbitrary",)),
        name="mixer_in",
    )(x, *weights, cos_t, sin_t)


def _attn_kernel(q_ref, k_ref, vt_ref, o_ref, sa_ref, sb_ref, mxa_ref, mxb_ref, m_ref, acc_ref, *, tq, tk):
    qi = pl.program_id(2)
    cw = ATTN_COLS
    ncol = tq // cw
    nkc = tk // LANES
    rw = min(ATTN_ROWS, tk)
    ones_rows = (lax.broadcasted_iota(jnp.int32, (SUM_ROWS, rw), 0) == 0).astype(BF16)
    ping = (sa_ref, mxa_ref)
    pong = (sb_ref, mxb_ref)

    def score_pass(j, c, dst, masked):
        s_ref, mx_ref = dst
        qc = q_ref[pl.ds(pl.multiple_of(c * cw, cw), cw), :]
        mx = None
        for r in range(tk // rw):
            if masked and r * rw >= (c + 1) * cw:
                continue
            kb = k_ref[pl.ds(pl.multiple_of(j * tk + r * rw, rw), rw), :]
            st = _mm_nt(kb, qc)
            if masked and (r + 1) * rw > c * cw + 1:
                kpos = r * rw + lax.broadcasted_iota(jnp.int32, (rw, cw), 0)
                qpos = c * cw + lax.broadcasted_iota(jnp.int32, (rw, cw), 1)
                st = jnp.where(kpos <= qpos, st, NEG_BIG)
            s_ref[c, r * rw:(r + 1) * rw, :] = st
            mr = jnp.max(st, axis=0, keepdims=True)
            mx = mr if mx is None else jnp.maximum(mx, mr)
        mx_ref[c] = mx

    def value_pass(j, c, src, diag=False):
        s_ref, mx_ref = src
        m_old = m_ref[c]
        m_new = jnp.maximum(m_old, mx_ref[c])
        pv = None
        for r in range(tk // rw):
            if diag and r * rw >= (c + 1) * cw:
                continue
            p = jnp.exp2(s_ref[c, r * rw:(r + 1) * rw, :] - m_new).astype(BF16)
            vb = jnp.concatenate([vt_ref[j * nkc + r * (rw // LANES) + i] for i in range(rw // LANES)], axis=1)
            vb = jnp.concatenate([vb, ones_rows], axis=0)
            part = _mm(vb, p)
            pv = part if pv is None else pv + part
        acc_ref[c] = jnp.exp2(m_old - m_new) * acc_ref[c] + pv
        m_ref[c] = m_new

    def over_cols(fn, static=False):
        if static:
            for c in range(ncol):
                fn(c)
            return

        def body(c, carry):
            fn(c)
            return carry
        lax.fori_loop(0, ncol, body, 0, unroll=ATTN_UNROLL)

    def step(j, src, dst, masked):
        def fn(c):
            value_pass(j, c, src)
            score_pass(j + 1, c, dst, masked)
        over_cols(fn, static=masked)

    m_ref[...] = jnp.full(m_ref.shape, NEG_BIG, F32)
    acc_ref[...] = jnp.zeros(acc_ref.shape, F32)

    @pl.when(qi == 0)
    def _():
        over_cols(lambda c: score_pass(0, c, ping, True), static=True)

    @pl.when(qi > 0)
    def _():
        over_cols(lambda c: score_pass(0, c, ping, False))

    n_plain = jnp.maximum(qi - 1, 0)

    def pair(i, carry):
        step(2 * i, ping, pong, False)
        step(2 * i + 1, pong, ping, False)
        return carry

    lax.fori_loop(0, n_plain // 2, pair, 0)

    @pl.when(n_plain % 2 == 1)
    def _():
        step(n_plain - 1, ping, pong, False)

    @pl.when(jnp.logical_and(qi > 0, qi % 2 == 1))
    def _():
        step(qi - 1, ping, pong, True)
        over_cols(lambda c: value_pass(qi, c, pong, diag=True), static=True)

    @pl.when(jnp.logical_and(qi > 0, qi % 2 == 0))
    def _():
        step(qi - 1, pong, ping, True)

    @pl.when(qi % 2 == 0)
    def _():
        over_cols(lambda c: value_pass(qi, c, ping, diag=True), static=True)

    for c in range(ncol):
        acc = acc_ref[c]
        out = acc[:V_HEAD] / acc[V_HEAD:V_HEAD + 1]
        for i in range(cw // LANES):
            o_ref[c * (cw // LANES) + i] = out[:, i * LANES:(i + 1) * LANES].astype(BF16)


def _attention(q, k, vt, batch, seq):
    tq = tk = min(ATTN_TILE, seq)
    nq = seq // tq
    ncol = tq // ATTN_COLS
    kern = functools.partial(_attn_kernel, tq=tq, tk=tk)
    return pl.pallas_call(
        kern,
        grid=(batch, N_HEADS, nq),
        in_specs=[pl.BlockSpec((tq, HEAD_PAD), lambda b, h, i: (b * nq + i, h)),
                  pl.BlockSpec((seq, HEAD_PAD), lambda b, h, i: (b, h)),
                  pl.BlockSpec((seq // LANES, V_HEAD, LANES), lambda b, h, i: (b, h, 0))],
        out_specs=pl.BlockSpec((tq // LANES, V_HEAD, LANES), lambda b, h, i: (b * nq + i, h, 0)),
        out_shape=jax.ShapeDtypeStruct((batch * seq // LANES, N_HEADS * V_HEAD, LANES), BF16),
        scratch_shapes=[pltpu.VMEM((ncol, tk, ATTN_COLS), F32),
                        pltpu.VMEM((ncol, tk, ATTN_COLS), F32),
                        pltpu.VMEM((ncol, 1, ATTN_COLS), F32),
                        pltpu.VMEM((ncol, 1, ATTN_COLS), F32),
                        pltpu.VMEM((ncol, 1, ATTN_COLS), F32),
                        pltpu.VMEM((ncol, V_HEAD + SUM_ROWS, ATTN_COLS), F32)],
        compiler_params=_cparams(("arbitrary", "arbitrary", "arbitrary")),
        name="attention",
    )(q, k, vt)


def _ssm_tables_kernel(lrr_ref, lir_ref, lrc_ref, lic_ref, ldt_ref, btr_ref, bti_ref,
                       cr_ref, ci_ref, ctr_ref, cti_ref,
                       klag_ref, win_ref, wout_ref, apr_ref, api_ref):
    ch = SSM_CHUNK
    dt = jnp.exp(ldt_ref[0])

    def power(lr, li, n):
        mag = jnp.exp(lr * dt * n)
        return mag * jnp.cos(li * dt * n), mag * jnp.sin(li * dt * n)

    lr = jnp.minimum(lrr_ref[0], LAM_RE_MAX)
    li = lir_ref[0]
    a_re, a_im = power(lr, li, 1.0)
    den = lr * lr + li * li
    nr, ni = a_re - 1.0, a_im
    coef_re = (nr * lr + ni * li) / den
    coef_im = (ni * lr - nr * li) / den
    btr, bti = btr_ref[0], bti_ref[0]
    bbr = coef_re * btr - coef_im * bti
    bbi = coef_re * bti + coef_im * btr
    cr, ci = cr_ref[0], ci_ref[0]
    f_re = jnp.concatenate([cr * bbr[i:i + 1] - ci * bbi[i:i + 1] for i in range(SSM_GROUP)], axis=0)
    f_im = jnp.concatenate([cr * bbi[i:i + 1] + ci * bbr[i:i + 1] for i in range(SSM_GROUP)], axis=0)

    lrc = jnp.minimum(lrc_ref[0], LAM_RE_MAX)
    lic = lic_ref[0]
    lag = lax.broadcasted_iota(jnp.int32, (1, ch), 1).astype(F32)
    pl_re, pl_im = power(lrc, lic, lag)
    hi = lax.Precision.HIGHEST
    klag_ref[0] = (jnp.dot(f_re, pl_re, precision=hi, preferred_element_type=F32)
                   - jnp.dot(f_im, pl_im, precision=hi, preferred_element_type=F32))

    back = (ch - 1) - lax.broadcasted_iota(jnp.int32, (ch, 1), 0).astype(F32)
    q_re, q_im = power(lr, li, back)
    for i in range(SSM_GROUP):
        w_re = q_re * bbr[i:i + 1] - q_im * bbi[i:i + 1]
        w_im = q_re * bbi[i:i + 1] + q_im * bbr[i:i + 1]
        win_ref[0, i * ch:(i + 1) * ch, :] = jnp.concatenate([w_re, w_im], axis=1).astype(BF16)

    pw_re, pw_im = power(lrc, lic, lag + 1.0)
    ctr, cti = ctr_ref[0], cti_ref[0]
    p = SSM_STATE
    for o in range(SSM_GROUP):
        c_re, c_im = ctr[:, o:o + 1], cti[:, o:o + 1]
        wout_ref[0, :p, o * ch:(o + 1) * ch] = (c_re * pw_re - c_im * pw_im).astype(BF16)
        wout_ref[0, p:, o * ch:(o + 1) * ch] = (-(c_re * pw_im + c_im * pw_re)).astype(BF16)

    apr_ref[0], api_ref[0] = power(lr, li, float(ch))


def _ssm_tables(lam_re, lam_im, log_dt, b_re, b_im, c_re, c_im):
    n = lam_re.shape[0]
    p, g, ch = SSM_STATE, SSM_GROUP, SSM_CHUNK
    args = [lam_re.reshape(n, 1, p), lam_im.reshape(n, 1, p),
            lam_re.reshape(n, p, 1), lam_im.reshape(n, p, 1), log_dt.reshape(n, 1, 1),
            jnp.swapaxes(b_re, 1, 2), jnp.swapaxes(b_im, 1, 2), c_re, c_im,
            jnp.swapaxes(c_re, 1, 2), jnp.swapaxes(c_im, 1, 2)]
    spec = lambda a: pl.BlockSpec((1,) + a.shape[1:], lambda i: (i, 0, 0))
    out_shape = [jax.ShapeDtypeStruct((n, g * g, ch), F32),
                 jax.ShapeDtypeStruct((n, g * ch, 2 * p), BF16),
                 jax.ShapeDtypeStruct((n, 2 * p, g * ch), BF16),
                 jax.ShapeDtypeStruct((n, 1, p), F32),
                 jax.ShapeDtypeStruct((n, 1, p), F32)]
    return pl.pallas_call(
        _ssm_tables_kernel,
        grid=(n,),
        in_specs=[spec(a) for a in args],
        out_specs=[spec(s) for s in out_shape],
        out_shape=out_shape,
        compiler_params=_cparams(("arbitrary",)),
        name="ssm_tables",
    )(*args)


def _ssm_kernel(u_ref, klag_ref, win_ref, wout_ref, apr_ref, api_ref, d_ref, s_ref,
                mt_ref, hin_re, hin_im, hst_re, hst_im, *, batch):
    ch, g, p = SSM_CHUNK, SSM_GROUP, SSM_STATE
    nc = u_ref.shape[0]
    ncb = nc // batch
    causal = (lax.broadcasted_iota(jnp.int32, (ch, ch), 1)
              >= lax.broadcasted_iota(jnp.int32, (ch, ch), 0))

    def build(i, _):
        for o in range(g):
            kv = klag_ref[0, pl.ds(i * g + o, 1), :]
            tz = pltpu.roll(jnp.broadcast_to(kv, (ch, ch)), 0, 1, stride=1, stride_axis=0)
            mt_ref[pl.ds(pl.multiple_of(i * ch, ch), ch), o * ch:(o + 1) * ch] = (
                jnp.where(causal, tz, 0.0).astype(BF16))
        return 0

    lax.fori_loop(0, g, build, 0)

    u2 = u_ref.reshape(nc * g, LANES)
    s2 = s_ref.reshape(nc * g, LANES)
    lhs = jnp.concatenate([u2[pl.ds(i, nc, stride=g), :].astype(BF16) for i in range(g)], axis=1)
    y = _mm(lhs, mt_ref[...])
    hin = _mm(lhs, win_ref[0])
    hin_re[...] = hin[:, :p]
    hin_im[...] = hin[:, p:]
    a_re, a_im = apr_ref[0], api_ref[0]

    def scan(c, carry):
        new = []
        for b in range(batch):
            h_re, h_im = carry[b]
            r = b * ncb + c
            hst_re[pl.ds(r, 1), :] = h_re
            hst_im[pl.ds(r, 1), :] = h_im
            x_re = hin_re[pl.ds(r, 1), :]
            x_im = hin_im[pl.ds(r, 1), :]
            new.append((a_re * h_re - a_im * h_im + x_re, a_re * h_im + a_im * h_re + x_im))
        return tuple(new)

    zero = jnp.zeros((1, p), F32)
    lax.fori_loop(0, ncb, scan, tuple((zero, zero) for _ in range(batch)))
    y = y + _mm(hst_re[...].astype(BF16), wout_ref[0, :p, :]) + _mm(hst_im[...].astype(BF16), wout_ref[0, p:, :])
    for o in range(g):
        yo = y[:, o * ch:(o + 1) * ch] + u2[pl.ds(o, nc, stride=g), :] * d_ref[0, o:o + 1, :]
        s2[pl.ds(o, nc, stride=g), :] = jax.nn.gelu(yo)


def _ssm(ut, klag, win, wout, apr, api, d_skip, layer, batch):
    nc = ut.shape[0]
    g, ch, p = SSM_GROUP, SSM_CHUNK, SSM_STATE
    base = layer * SSM_GROUPS
    tab = lambda a: pl.BlockSpec((1,) + a.shape[1:], lambda i: (base + i, 0, 0))
    kern = functools.partial(_ssm_kernel, batch=batch)
    return pl.pallas_call(
        kern,
        grid=(SSM_GROUPS,),
        in_specs=[pl.BlockSpec((nc, g, LANES), lambda i: (0, i, 0)),
                  tab(klag), tab(win), tab(wout), tab(apr), tab(api),
                  pl.BlockSpec((1, g, 1), lambda i: (i, 0, 0))],
        out_specs=pl.BlockSpec((nc, g, LANES), lambda i: (0, i, 0)),
        out_shape=jax.ShapeDtypeStruct(ut.shape, F32),
        scratch_shapes=[pltpu.VMEM((g * ch, g * ch), BF16),
                        pltpu.VMEM((nc, p), F32), pltpu.VMEM((nc, p), F32),
                        pltpu.VMEM((nc, p), F32), pltpu.VMEM((nc, p), F32)],
        compiler_params=_cparams(("arbitrary",)),
        name="ssm",
    )(ut, klag, win, wout, apr, api, d_skip.reshape(SSM_GROUPS, g, 1))


def _merge_kernel(yt_ref, st_ref, zc_ref, halo_ref, g_ref, x_ref,
                  woa_ref, wglut_ref, bglu_ref, wos_ref, cw_ref, cb_ref, lng_ref, lnb_ref,
                  woc_ref, wout_ref, nf_ref, *rest, tiles_per_seq, with_router):
    if with_router:
        wrh_ref, wrl_ref, xn_ref, h2_ref, route_ref = rest
    else:
        xn_ref, h2_ref = rest
    tm = x_ref.shape[0]
    nch = tm // LANES
    i = pl.program_id(0)

    yt = jnp.concatenate([yt_ref[c] for c in range(nch)], axis=1)
    y_attn = _mm_tn(yt, woa_ref[...])

    st = jnp.concatenate([st_ref[c] for c in range(nch)], axis=1)
    gate = jax.nn.sigmoid(_mm(wglut_ref[...], st.astype(BF16)) + bglu_ref[...])
    y_ssm = _mm_tn((st * gate).astype(BF16), wos_ref[...])

    first = (i % tiles_per_seq) == 0
    zs = jnp.concatenate([jnp.where(first, 0.0, halo_ref[...]), zc_ref[...]], axis=0)
    nz = CONV_HALO + tm
    off = CONV_HALO - (CONV_K - 1)
    conv = cb_ref[...]
    for res in range(SUBLANES):
        zr = zs if res == 0 else pltpu.roll(zs, nz - res, 0)
        for kk in range(CONV_K):
            if (off + kk) % SUBLANES == res:
                lo = off + kk - res
                conv = conv + zr[lo:lo + tm, :] * cw_ref[kk:kk + 1, :]
    mu = jnp.mean(conv, axis=-1, keepdims=True)
    cen = conv - mu
    var = jnp.mean(cen * cen, axis=-1, keepdims=True)
    yc = jax.nn.silu(cen * lax.rsqrt(var + LN_EPS) * lng_ref[...] + lnb_ref[...])
    y_conv = _mm(yc.astype(BF16), woc_ref[...])

    d = D_MODEL
    merged = (g_ref[:, :d].astype(F32) * y_attn + g_ref[:, d:2 * d].astype(F32) * y_ssm
              + g_ref[:, 2 * d:].astype(F32) * y_conv)
    xn = x_ref[...] + _mm(merged.astype(BF16), wout_ref[...])
    xn_ref[...] = xn
    h = _rms(xn, nf_ref[...])
    h2_ref[...] = h.astype(h2_ref.dtype)

    if with_router:
        h_hi = h.astype(BF16)
        h_lo = (h - h_hi.astype(F32)).astype(BF16)
        logits = _mm(h_hi, wrh_ref[...]) + _mm(h_lo, wrh_ref[...]) + _mm(h_hi, wrl_ref[...])
        lane = lax.broadcasted_iota(jnp.int32, (tm, LANES), 1)
        logits = jnp.where(lane < N_EXPERTS, logits, NEG_BIG)
        m1 = jnp.max(logits, axis=-1, keepdims=True)
        i1 = jnp.min(jnp.where(logits == m1, lane, LANES), axis=-1, keepdims=True)
        rest_l = jnp.where(lane == i1, NEG_BIG, logits)
        m2 = jnp.max(rest_l, axis=-1, keepdims=True)
        i2 = jnp.min(jnp.where(rest_l == m2, lane, LANES), axis=-1, keepdims=True)
        e = jnp.exp(m2 - m1)
        g1 = 1.0 / (1.0 + e)
        g2 = e / (1.0 + e)
        route_ref[...] = jnp.where(lane == 0, i1.astype(F32),
                         jnp.where(lane == 1, i2.astype(F32),
                         jnp.where(lane == 2, g1, jnp.where(lane == 3, g2, 0.0))))


def _merge(yt, st, zc, gates, x, w, seq, with_router):
    t = x.shape[0]
    tm = min(TOKEN_TILE, t, seq)
    nch = tm // LANES
    hpt = tm // CONV_HALO
    row = lambda c: pl.BlockSpec((tm, c), lambda i: (i, 0))
    chunked = lambda c: pl.BlockSpec((nch, c, LANES), lambda i: (i, 0, 0))
    weights = [w["woa"], w["wglut"], w["bglu"], w["wos"], w["cw"], w["cb"], w["lng"], w["lnb"],
               w["woc"], w["wout"], w["nf"]]
    out_shape = [jax.ShapeDtypeStruct((t, D_MODEL), F32),
                 jax.ShapeDtypeStruct((t, D_MODEL), F32 if with_router else BF16)]
    out_specs = [row(D_MODEL), row(D_MODEL)]
    if with_router:
        weights += [w["wrh"], w["wrl"]]
        out_shape.append(jax.ShapeDtypeStruct((t, LANES), F32))
        out_specs.append(row(LANES))
    kern = functools.partial(_merge_kernel, tiles_per_seq=seq // tm, with_router=with_router)
    return pl.pallas_call(
        kern,
        grid=(t // tm,),
        in_specs=[chunked(N_HEADS * V_HEAD), chunked(SSM_WIDTH), row(CONV_WIDTH),
                  pl.BlockSpec((CONV_HALO, CONV_WIDTH), lambda i: (jnp.maximum(i * hpt - 1, 0), 0)),
                  row(3 * D_MODEL), row(D_MODEL)] + [_const_spec(a.shape) for a in weights],
        out_specs=out_specs,
        out_shape=out_shape,
        compiler_params=_cparams(("arbitrary",)),
        name="merge_router" if with_router else "merge",
    )(yt, st, zc, zc, gates, x, *weights)


def _ffn_kernel(x_ref, h_ref, w1_ref, w3_ref, w2_ref, o_ref):
    h = h_ref[...]
    f = w1_ref.shape[1]
    acc = x_ref[...]
    for lo in range(0, f, FFN_F_TILE):
        hi = min(lo + FFN_F_TILE, f)
        a = _mm(h, w1_ref[:, lo:hi])
        b = _mm(h, w3_ref[:, lo:hi])
        acc = acc + _mm((jax.nn.silu(a) * b).astype(BF16), w2_ref[lo:hi, :])
    o_ref[...] = acc


def _ffn(x, h, w1, w3, w2):
    t = x.shape[0]
    tm = min(TOKEN_TILE, t)
    row = pl.BlockSpec((tm, D_MODEL), lambda i: (i, 0))
    return pl.pallas_call(
        _ffn_kernel,
        grid=(t // tm,),
        in_specs=[row, row, _const_spec(w1.shape), _const_spec(w3.shape), _const_spec(w2.shape)],
        out_specs=row,
        out_shape=jax.ShapeDtypeStruct((t, D_MODEL), F32),
        compiler_params=_cparams(("arbitrary",)),
        name="ffn",
    )(x, h, w1, w3, w2)


def _dispatch_kernel(dest_ref, h_ref, xs_in_ref, xs_ref, sem):
    del xs_in_ref
    tm = h_ref.shape[0]

    def row_copy(t, k):
        return pltpu.make_async_copy(h_ref.at[pl.ds(t, 1), :],
                                     xs_ref.at[pl.ds(dest_ref[0, 0, TOP_K * t + k], 1), :], sem)

    def issue(t, c):
        for k in range(TOP_K):
            row_copy(t, k).start(priority=k)
        return c

    def drain(t, c):
        for k in range(TOP_K):
            row_copy(t, k).wait()
        return c

    lax.fori_loop(0, tm, issue, 0, unroll=ROW_DMA_UNROLL)
    lax.fori_loop(0, tm, drain, 0, unroll=ROW_DMA_UNROLL)


def _dispatch(h, dest, rows):
    t = h.shape[0]
    tm = min(ROUTE_TILE, t)
    nt = t // tm
    xs0 = jnp.zeros((rows, D_MODEL), h.dtype)
    return pl.pallas_call(
        _dispatch_kernel,
        grid=(nt,),
        in_specs=[pl.BlockSpec((1, 1, TOP_K * tm), lambda i: (i, 0, 0), memory_space=pltpu.SMEM),
                  pl.BlockSpec((tm, D_MODEL), lambda i: (i, 0)),
                  pl.BlockSpec(memory_space=pl.ANY)],
        out_specs=pl.BlockSpec(memory_space=pl.ANY),
        out_shape=jax.ShapeDtypeStruct((rows, D_MODEL), h.dtype),
        scratch_shapes=[pltpu.SemaphoreType.DMA(())],
        input_output_aliases={2: 0},
        compiler_params=_cparams(("arbitrary",)),
        name="moe_dispatch",
    )(dest.reshape(nt, 1, TOP_K * tm), h, xs0)


def _experts_kernel(te_ref, nu_ref, x_ref, w1_ref, w3_ref, w2_ref, y_ref, act_ref):
    i = pl.program_id(0)
    nf, _, tf = act_ref.shape
    used = i < nu_ref[0]

    @pl.when(jnp.logical_not(used))
    def _():
        y_ref[...] = jnp.zeros_like(y_ref)

    @pl.when(used)
    def _():
        xb = x_ref[...].astype(BF16)
        for c in range(nf):
            a = _mm(xb, w1_ref[0, :, c * tf:(c + 1) * tf])
            b = _mm(xb, w3_ref[0, :, c * tf:(c + 1) * tf])
            act_ref[c] = (jax.nn.silu(a) * b).astype(BF16)
        act = jnp.concatenate([act_ref[c] for c in range(nf)], axis=1)
        y_ref[...] = _mm(act, w2_ref[0])


def _experts(xs, tile_expert, n_used, w1, w3, w2):
    rows = xs.shape[0]
    tm = MOE_ROW_TILE
    f = w1.shape[2]
    tf = min(MOE_F_TILE, f)
    nf = f // tf
    once = pl.Buffered(1)
    grid_spec = pltpu.PrefetchScalarGridSpec(
        num_scalar_prefetch=2,
        grid=(rows // tm,),
        in_specs=[pl.BlockSpec((tm, D_MODEL), lambda i, te, nu: (i, 0)),
                  pl.BlockSpec((1, D_MODEL, f), lambda i, te, nu: (te[i], 0, 0), pipeline_mode=once),
                  pl.BlockSpec((1, D_MODEL, f), lambda i, te, nu: (te[i], 0, 0), pipeline_mode=once),
                  pl.BlockSpec((1, f, D_MODEL), lambda i, te, nu: (te[i], 0, 0), pipeline_mode=once)],
        out_specs=pl.BlockSpec((tm, D_MODEL), lambda i, te, nu: (i, 0)),
        scratch_shapes=[pltpu.VMEM((nf, tm, tf), BF16)],
    )
    return pl.pallas_call(
        _experts_kernel,
        grid_spec=grid_spec,
        out_shape=jax.ShapeDtypeStruct((rows, D_MODEL), F32),
        compiler_params=_cparams(("arbitrary",)),
        name="moe_experts",
    )(tile_expert, n_used, xs, w1, w3, w2)


def _combine_kernel(dest_ref, x_ref, route_ref, nfin_ref, ys_ref, o_ref, buf_ref, sem):
    tm = x_ref.shape[0]

    def row_copy(t, k):
        return pltpu.make_async_copy(ys_ref.at[pl.ds(dest_ref[0, 0, TOP_K * t + k], 1), :],
                                     buf_ref.at[pl.ds(k * tm + t, 1), :], sem)

    def issue(t, c):
        for k in range(TOP_K):
            row_copy(t, k).start(priority=k)
        return c

    def drain(t, c):
        for k in range(TOP_K):
            row_copy(t, k).wait()
        return c

    lax.fori_loop(0, tm, issue, 0, unroll=ROW_DMA_UNROLL)
    lax.fori_loop(0, tm, drain, 0, unroll=ROW_DMA_UNROLL)
    route = route_ref[...]
    y = x_ref[...] + route[:, 2:3] * buf_ref[:tm, :] + route[:, 3:4] * buf_ref[tm:, :]
    o_ref[...] = _rms(y, nfin_ref[...])


def _combine(x, route, dest, ys, norm_final):
    t = x.shape[0]
    tm = min(ROUTE_TILE, t)
    nt = t // tm
    return pl.pallas_call(
        _combine_kernel,
        grid=(nt,),
        in_specs=[pl.BlockSpec((1, 1, TOP_K * tm), lambda i: (i, 0, 0), memory_space=pltpu.SMEM),
                  pl.BlockSpec((tm, D_MODEL), lambda i: (i, 0)),
                  pl.BlockSpec((tm, LANES), lambda i: (i, 0)),
                  _const_spec((1, D_MODEL)),
                  pl.BlockSpec(memory_space=pl.ANY)],
        out_specs=pl.BlockSpec((tm, D_MODEL), lambda i: (i, 0)),
        out_shape=jax.ShapeDtypeStruct((t, D_MODEL), F32),
        scratch_shapes=[pltpu.VMEM((TOP_K * tm, D_MODEL), F32), pltpu.SemaphoreType.DMA(())],
        compiler_params=_cparams(("arbitrary",)),
        name="moe_combine",
    )(dest.reshape(nt, 1, TOP_K * tm), x, route, norm_final.reshape(1, D_MODEL), ys)


def _moe_plan(route, row_tile):
    t = route.shape[0]
    m = t * TOP_K
    e_flat = route[:, :TOP_K].astype(jnp.int32).reshape(m)
    onehot = (e_flat[:, None] == jnp.arange(N_EXPERTS, dtype=jnp.int32)[None, :]).astype(jnp.int32)
    csum = jnp.cumsum(onehot, axis=0)
    rank = jnp.sum((csum - onehot) * onehot, axis=1)
    counts = csum[-1]
    padded = ((counts + row_tile - 1) // row_tile) * row_tile
    pad_ends = jnp.cumsum(padded)
    pad_starts = pad_ends - padded
    dest = jnp.sum(onehot * pad_starts[None, :], axis=1) + rank
    n_tiles = -(-m // row_tile) + N_EXPERTS
    tile_start = jnp.arange(n_tiles, dtype=jnp.int32) * row_tile
    tile_expert = jnp.minimum(jnp.sum((pad_ends[None, :] <= tile_start[:, None]).astype(jnp.int32), axis=1),
                              N_EXPERTS - 1)
    n_used = (pad_ends[-1] // row_tile).reshape(1)
    return dest.astype(jnp.int32), tile_expert.astype(jnp.int32), n_used.astype(jnp.int32), n_tiles * row_tile


def _prep_mixer_weights(l, w_in, q_norm, w_uq, kv_norm, w_ukv, w_o_attn, ssm_w_glu, ssm_b_glu, w_o_ssm,
                        conv_w, conv_b, conv_ln_g, conv_ln_b, w_o_conv, w_out, norm_ffn):
    o = IN_OFFSETS
    wi = w_in[l]
    low = wi[:, o[0]:o[3]]
    wa = jnp.pad(low, ((0, 0), (0, 4 * LANES - low.shape[1]))).astype(BF16)
    wq = jnp.pad(w_uq[l].reshape(Q_LORA, N_HEADS, QK_HEAD),
                 ((0, 0), (0, 0), (0, HEAD_PAD - QK_HEAD))).reshape(Q_LORA, N_HEADS * HEAD_PAD).astype(BF16)
    wkv = w_ukv[l].reshape(KV_LORA, N_HEADS, QK_NOPE + V_HEAD)
    wkn = jnp.pad(wkv[..., :QK_NOPE], ((0, 0), (0, 0), (0, HEAD_PAD - QK_NOPE))
                  ).reshape(KV_LORA, N_HEADS * HEAD_PAD).astype(BF16)
    wvt = wkv[..., QK_NOPE:].reshape(KV_LORA, N_HEADS * V_HEAD).T.astype(BF16)
    place = np.zeros((LANES, N_HEADS * HEAD_PAD), np.float32)
    for h in range(N_HEADS):
        for r in range(QK_ROPE):
            place[r, h * HEAD_PAD + QK_NOPE + r] = 1.0
    return dict(
        wa=wa, wut=wi[:, o[3]:o[4]].T.astype(BF16), wca=wi[:, o[4]:o[5]].astype(BF16),
        wcg=wi[:, o[5]:o[6]].astype(BF16), wg=wi[:, o[6]:o[7]].astype(BF16),
        qn=q_norm[l].reshape(1, Q_LORA), kvn=kv_norm[l].reshape(1, KV_LORA),
        wq=wq, wkn=wkn, wke=jnp.asarray(place, BF16), wvt=wvt,
        woa=w_o_attn[l].astype(BF16), wglut=ssm_w_glu[l].T.astype(BF16),
        bglu=ssm_b_glu[l].reshape(SSM_WIDTH, 1), wos=w_o_ssm[l].astype(BF16),
        cw=conv_w[l], cb=conv_b[l].reshape(1, CONV_WIDTH), lng=conv_ln_g[l].reshape(1, CONV_WIDTH),
        lnb=conv_ln_b[l].reshape(1, CONV_WIDTH), woc=w_o_conv[l].astype(BF16),
        wout=w_out[l].astype(BF16), nf=norm_ffn[l].reshape(1, D_MODEL))


def kernel(x, positions, norm_mix, w_in, q_norm, w_uq, kv_norm, w_ukv, w_o_attn, ssm_lam_re, ssm_lam_im, ssm_log_dt, ssm_b_re, ssm_b_im, ssm_c_re, ssm_c_im, ssm_d, ssm_w_glu, ssm_b_glu, w_o_ssm, conv_w, conv_b, conv_ln_g, conv_ln_b, w_o_conv, w_out, norm_ffn, ffn_w1, ffn_w3, ffn_w2, moe_router, moe_w1, moe_w3, moe_w2, norm_final):
    batch, seq, d = x.shape
    depth = w_in.shape[0]
    assert d == D_MODEL and depth == 2 and seq % SSM_CHUNK == 0
    t = batch * seq
    xf = x.reshape(t, d)
    cos_t, sin_t = _rope_tables(positions)
    ng = depth * SSM_GROUPS
    klag, win, wout_s, apr, api = _ssm_tables(
        ssm_lam_re.reshape(ng, SSM_STATE), ssm_lam_im.reshape(ng, SSM_STATE), ssm_log_dt.reshape(ng),
        ssm_b_re.reshape(ng, SSM_STATE, SSM_GROUP), ssm_b_im.reshape(ng, SSM_STATE, SSM_GROUP),
        ssm_c_re.reshape(ng, SSM_GROUP, SSM_STATE), ssm_c_im.reshape(ng, SSM_GROUP, SSM_STATE))

    for layer in range(depth):
        moe_layer = layer % 2 == 1
        w = _prep_mixer_weights(layer, w_in, q_norm, w_uq, kv_norm, w_ukv, w_o_attn, ssm_w_glu, ssm_b_glu,
                                w_o_ssm, conv_w, conv_b, conv_ln_g, conv_ln_b, w_o_conv, w_out, norm_ffn)
        q, k, vt, ut, zc, gates = _mixer_in(xf, norm_mix[layer].reshape(1, d), w, cos_t, sin_t)
        yt = _attention(q, k, vt, batch, seq)
        st = _ssm(ut, klag, win, wout_s, apr, api, ssm_d[layer], layer, batch)
        i = layer // 2
        if not moe_layer:
            xn, h2 = _merge(yt, st, zc, gates, xf, w, seq, with_router=False)
            xf = _ffn(xn, h2, ffn_w1[i].astype(BF16), ffn_w3[i].astype(BF16), ffn_w2[i].astype(BF16))
        else:
            wr = jnp.pad(moe_router[i], ((0, 0), (0, LANES - N_EXPERTS)))
            w["wrh"] = wr.astype(BF16)
            w["wrl"] = (wr - w["wrh"].astype(F32)).astype(BF16)
            xn, h2, route = _merge(yt, st, zc, gates, xf, w, seq, with_router=True)
            dest, tile_expert, n_used, rows = _moe_plan(route, MOE_ROW_TILE)
            xs = _dispatch(h2, dest, rows)
            ys = _experts(xs, tile_expert, n_used, moe_w1[i].astype(BF16), moe_w3[i].astype(BF16),
                          moe_w2[i].astype(BF16))
            xf = _combine(xn, route, dest, ys, norm_final)
    return xf.reshape(batch, seq, d)
```

```python
import functools

import numpy as np
import jax
import jax.numpy as jnp
from jax import lax
from jax.experimental import pallas as pl
from jax.experimental.pallas import tpu as pltpu

F32 = jnp.float32
BF16 = jnp.bfloat16

D_MODEL = 1024
N_HEADS = 8
Q_LORA = 256
KV_LORA = 128
QK_NOPE = 64
QK_ROPE = 32
QK_HEAD = QK_NOPE + QK_ROPE
V_HEAD = 64
ROPE_THETA = 10000.0
SSM_WIDTH = 512
SSM_GROUP = 16
SSM_GROUPS = SSM_WIDTH // SSM_GROUP
SSM_STATE = 64
SSM_CHUNK = 128
CONV_WIDTH = 512
CONV_K = 31
N_EXPERTS = 8
TOP_K = 2
RMS_EPS = 1e-6
LN_EPS = 1e-5
LAM_RE_MAX = -1e-4
IN_SPLITS = (Q_LORA, KV_LORA, QK_ROPE, SSM_WIDTH, CONV_WIDTH, CONV_WIDTH, 3 * D_MODEL)
IN_OFFSETS = tuple(int(v) for v in np.cumsum((0,) + IN_SPLITS))

LANES = 128
SUBLANES = 8
HEAD_PAD = LANES
VMEM_LIMIT_BYTES = 56 * 1024 * 1024

TOKEN_TILE = 512
ATTN_TILE = 1024
ATTN_COLS = 256
ATTN_ROWS = 256
ATTN_UNROLL = 4
MOE_ROW_TILE = 1024
MOE_F_TILE = 512
FFN_F_TILE = 512
FFN_ROW_TILE = 1024
ROUTE_TILE = 1024
ROPE_TILE = 2048
ROW_DMA_UNROLL = 8
CONV_HALO = 32
SUM_ROWS = 16
NEG_BIG = -1e30
LOG2_E = float(np.log2(np.e))


def _cparams(sem):
    return pltpu.CompilerParams(dimension_semantics=sem, vmem_limit_bytes=VMEM_LIMIT_BYTES)


def _const_spec(shape):
    nd = len(shape)
    return pl.BlockSpec(shape, lambda *_: (0,) * nd, pipeline_mode=pl.Buffered(1))


def _rms(x, g):
    return x * lax.rsqrt(jnp.mean(x * x, axis=-1, keepdims=True) + RMS_EPS) * g


def _mm(a, b):
    return jnp.dot(a, b, preferred_element_type=F32)


def _mm_nt(a, b):
    return lax.dot_general(a, b, (((1,), (1,)), ((), ())), preferred_element_type=F32)


def _mm_tn(a, b):
    return lax.dot_general(a, b, (((0,), (0,)), ((), ())), preferred_element_type=F32)


def _rope_tables_kernel(pos_ref, invf_ref, sign_ref, cos_ref, sin_ref):
    ang = pos_ref[...].astype(F32) * invf_ref[...]
    cos_ref[...] = jnp.cos(ang)
    sin_ref[...] = jnp.sin(ang) * sign_ref[...]


def _rope_tables(positions):
    t = positions.size
    tm = min(ROPE_TILE, t)
    half = QK_ROPE // 2
    inv_freq = ROPE_THETA ** (-np.arange(0, QK_ROPE, 2, dtype=np.float32) / QK_ROPE)
    invf = np.zeros((1, LANES), np.float32)
    sign = np.zeros((1, LANES), np.float32)
    invf[0, QK_NOPE:QK_NOPE + half] = inv_freq
    invf[0, QK_NOPE + half:QK_HEAD] = inv_freq
    sign[0, QK_NOPE:QK_NOPE + half] = -1.0
    sign[0, QK_NOPE + half:QK_HEAD] = 1.0
    return pl.pallas_call(
        _rope_tables_kernel,
        grid=(t // tm,),
        in_specs=[pl.BlockSpec((tm, 1), lambda i: (i, 0)),
                  pl.BlockSpec((1, LANES), lambda i: (0, 0)),
                  pl.BlockSpec((1, LANES), lambda i: (0, 0))],
        out_specs=[pl.BlockSpec((tm, LANES), lambda i: (i, 0))] * 2,
        out_shape=[jax.ShapeDtypeStruct((t, LANES), F32)] * 2,
        compiler_params=_cparams(("arbitrary",)),
        name="rope_tables",
    )(positions.reshape(t, 1), jnp.asarray(invf), jnp.asarray(sign))


def _rope_head(xh, cos, sin, lane):
    half = QK_ROPE // 2
    swapped = jnp.where(lane < QK_NOPE + half,
                        pltpu.roll(xh, LANES - half, 1), pltpu.roll(xh, half, 1))
    return xh * cos + swapped * sin


def _mixer_in_kernel(x_ref, nm_ref, wa_ref, wut_ref, wca_ref, wcg_ref, wg_ref, qn_ref, kvn_ref,
                     wq_ref, wkn_ref, wke_ref, wvt_ref, cos_ref, sin_ref,
                     q_ref, k_ref, vt_ref, ut_ref, zc_ref, g_ref):
    tm = x_ref.shape[0]
    hn = _rms(x_ref[...], nm_ref[...]).astype(BF16)
    za = _mm(hn, wa_ref[...])
    cqn = _rms(za[:, :Q_LORA], qn_ref[...]).astype(BF16)
    ckvn = _rms(za[:, Q_LORA:Q_LORA + KV_LORA], kvn_ref[...]).astype(BF16)
    kpe = za[:, Q_LORA + KV_LORA:].astype(BF16)
    q = _mm(cqn, wq_ref[...])
    k = _mm(ckvn, wkn_ref[...]) + _mm(kpe, wke_ref[...])
    cos = cos_ref[...]
    sin = sin_ref[...]
    lane = lax.broadcasted_iota(jnp.int32, (tm, LANES), 1)
    scale = QK_HEAD ** -0.5 * LOG2_E
    for h in range(N_HEADS):
        sl = slice(h * HEAD_PAD, (h + 1) * HEAD_PAD)
        q_ref[:, sl] = (_rope_head(q[:, sl], cos, sin, lane) * scale).astype(BF16)
        k_ref[:, sl] = _rope_head(k[:, sl], cos, sin, lane).astype(BF16)
    vt = _mm_nt(wvt_ref[...], ckvn)
    ut = _mm_nt(wut_ref[...], hn)
    for c in range(tm // LANES):
        sl = slice(c * LANES, (c + 1) * LANES)
        vt_ref[c] = vt[:, sl].astype(BF16)
        ut_ref[c] = ut[:, sl]
    zc_ref[...] = _mm(hn, wca_ref[...]) * jax.nn.sigmoid(_mm(hn, wcg_ref[...]))
    for c in range(3):
        sl = slice(c * D_MODEL, (c + 1) * D_MODEL)
        g_ref[:, sl] = jax.nn.sigmoid(_mm(hn, wg_ref[:, sl])).astype(BF16)


def _mixer_in(x, nm, w, cos_t, sin_t):
    t = x.shape[0]
    tm = min(TOKEN_TILE, t)
    nch = tm // LANES
    row = lambda c: pl.BlockSpec((tm, c), lambda i: (i, 0))
    chunked = lambda c: pl.BlockSpec((nch, c, LANES), lambda i: (i, 0, 0))
    weights = [nm, w["wa"], w["wut"], w["wca"], w["wcg"], w["wg"], w["qn"], w["kvn"],
               w["wq"], w["wkn"], w["wke"], w["wvt"]]
    return pl.pallas_call(
        _mixer_in_kernel,
        grid=(t // tm,),
        in_specs=[row(D_MODEL)] + [_const_spec(a.shape) for a in weights] + [row(LANES), row(LANES)],
        out_specs=[row(N_HEADS * HEAD_PAD), row(N_HEADS * HEAD_PAD), chunked(N_HEADS * V_HEAD),
                   chunked(SSM_WIDTH), row(CONV_WIDTH), row(3 * D_MODEL)],
        out_shape=[jax.ShapeDtypeStruct((t, N_HEADS * HEAD_PAD), BF16),
                   jax.ShapeDtypeStruct((t, N_HEADS * HEAD_PAD), BF16),
                   jax.ShapeDtypeStruct((t // LANES, N_HEADS * V_HEAD, LANES), BF16),
                   jax.ShapeDtypeStruct((t // LANES, SSM_WIDTH, LANES), F32),
                   jax.ShapeDtypeStruct((t, CONV_WIDTH), F32),
                   jax.ShapeDtypeStruct((t, 3 * D_MODEL), BF16)],
        compiler_params=_cparams(("arbitrary",)),
        name="mixer_in",
    )(x, *weights, cos_t, sin_t)


def _attn_kernel(q_ref, k_ref, vt_ref, o_ref, sa_ref, sb_ref, mxa_ref, mxb_ref, m_ref, acc_ref, *, tq, tk):
    qi = pl.program_id(2)
    cw = ATTN_COLS
    ncol = tq // cw
    nkc = tk // LANES
    rw = min(ATTN_ROWS, tk)
    ones_rows = (lax.broadcasted_iota(jnp.int32, (SUM_ROWS, rw), 0) == 0).astype(BF16)
    ping = (sa_ref, mxa_ref)
    pong = (sb_ref, mxb_ref)

    def score_pass(j, c, dst, masked):
        s_ref, mx_ref = dst
        qc = q_ref[pl.ds(pl.multiple_of(c * cw, cw), cw), :]
        mx = None
        for r in range(tk // rw):
            if masked and r * rw >= (c + 1) * cw:
                continue
            kb = k_ref[pl.ds(pl.multiple_of(j * tk + r * rw, rw), rw), :]
            st = _mm_nt(kb, qc)
            if masked and (r + 1) * rw > c * cw + 1:
                kpos = r * rw + lax.broadcasted_iota(jnp.int32, (rw, cw), 0)
                qpos = c * cw + lax.broadcasted_iota(jnp.int32, (rw, cw), 1)
                st = jnp.where(kpos <= qpos, st, NEG_BIG)
            s_ref[c, r * rw:(r + 1) * rw, :] = st
            mr = jnp.max(st, axis=0, keepdims=True)
            mx = mr if mx is None else jnp.maximum(mx, mr)
        mx_ref[c] = mx

    def value_pass(j, c, src, diag=False):
        s_ref, mx_ref = src
        m_old = m_ref[c]
        m_new = jnp.maximum(m_old, mx_ref[c])
        pv = None
        for r in range(tk // rw):
            if diag and r * rw >= (c + 1) * cw:
                continue
            p = jnp.exp2(s_ref[c, r * rw:(r + 1) * rw, :] - m_new).astype(BF16)
            vb = jnp.concatenate([vt_ref[j * nkc + r * (rw // LANES) + i] for i in range(rw // LANES)], axis=1)
            vb = jnp.concatenate([vb, ones_rows], axis=0)
            part = _mm(vb, p)
            pv = part if pv is None else pv + part
        acc_ref[c] = jnp.exp2(m_old - m_new) * acc_ref[c] + pv
        m_ref[c] = m_new

    def over_cols(fn, static=False):
        if static:
            for c in range(ncol):
                fn(c)
            return

        def body(c, carry):
            fn(c)
            return carry
        lax.fori_loop(0, ncol, body, 0, unroll=ATTN_UNROLL)

    def step(j, src, dst, masked):
        def fn(c):
            value_pass(j, c, src)
            score_pass(j + 1, c, dst, masked)
        over_cols(fn, static=masked)

    m_ref[...] = jnp.full(m_ref.shape, NEG_BIG, F32)
    acc_ref[...] = jnp.zeros(acc_ref.shape, F32)

    @pl.when(qi == 0)
    def _():
        over_cols(lambda c: score_pass(0, c, ping, True), static=True)

    @pl.when(qi > 0)
    def _():
        over_cols(lambda c: score_pass(0, c, ping, False))

    n_plain = jnp.maximum(qi - 1, 0)

    def pair(i, carry):
        step(2 * i, ping, pong, False)
        step(2 * i + 1, pong, ping, False)
        return carry

    lax.fori_loop(0, n_plain // 2, pair, 0)

    @pl.when(n_plain % 2 == 1)
    def _():
        step(n_plain - 1, ping, pong, False)

    @pl.when(jnp.logical_and(qi > 0, qi % 2 == 1))
    def _():
        step(qi - 1, ping, pong, True)
        over_cols(lambda c: value_pass(qi, c, pong, diag=True), static=True)

    @pl.when(jnp.logical_and(qi > 0, qi % 2 == 0))
    def _():
        step(qi - 1, pong, ping, True)

    @pl.when(qi % 2 == 0)
    def _():
        over_cols(lambda c: value_pass(qi, c, ping, diag=True), static=True)

    for c in range(ncol):
        acc = acc_ref[c]
        out = acc[:V_HEAD] / acc[V_HEAD:V_HEAD + 1]
        for i in range(cw // LANES):
            o_ref[c * (cw // LANES) + i] = out[:, i * LANES:(i + 1) * LANES].astype(BF16)


def _attention(q, k, vt, batch, seq):
    tq = tk = min(ATTN_TILE, seq)
    nq = seq // tq
    ncol = tq // ATTN_COLS
    kern = functools.partial(_attn_kernel, tq=tq, tk=tk)
    return pl.pallas_call(
        kern,
        grid=(batch, N_HEADS, nq),
        in_specs=[pl.BlockSpec((tq, HEAD_PAD), lambda b, h, i: (b * nq + i, h)),
                  pl.BlockSpec((seq, HEAD_PAD), lambda b, h, i: (b, h)),
                  pl.BlockSpec((seq // LANES, V_HEAD, LANES), lambda b, h, i: (b, h, 0))],
        out_specs=pl.BlockSpec((tq // LANES, V_HEAD, LANES), lambda b, h, i: (b * nq + i, h, 0)),
        out_shape=jax.ShapeDtypeStruct((batch * seq // LANES, N_HEADS * V_HEAD, LANES), BF16),
        scratch_shapes=[pltpu.VMEM((ncol, tk, ATTN_COLS), F32),
                        pltpu.VMEM((ncol, tk, ATTN_COLS), F32),
                        pltpu.VMEM((ncol, 1, ATTN_COLS), F32),
                        pltpu.VMEM((ncol, 1, ATTN_COLS), F32),
                        pltpu.VMEM((ncol, 1, ATTN_COLS), F32),
                        pltpu.VMEM((ncol, V_HEAD + SUM_ROWS, ATTN_COLS), F32)],
        compiler_params=_cparams(("arbitrary", "arbitrary", "arbitrary")),
        name="attention",
    )(q, k, vt)


def _ssm_tables_kernel(lrr_ref, lir_ref, lrc_ref, lic_ref, ldt_ref, btr_ref, bti_ref,
                       cr_ref, ci_ref, ctr_ref, cti_ref,
                       klag_ref, win_ref, wout_ref, apr_ref, api_ref):
    ch = SSM_CHUNK
    dt = jnp.exp(ldt_ref[0])

    def power(lr, li, n):
        mag = jnp.exp(lr * dt * n)
        return mag * jnp.cos(li * dt * n), mag * jnp.sin(li * dt * n)

    lr = jnp.minimum(lrr_ref[0], LAM_RE_MAX)
    li = lir_ref[0]
    a_re, a_im = power(lr, li, 1.0)
    den = lr * lr + li * li
    nr, ni = a_re - 1.0, a_im
    coef_re = (nr * lr + ni * li) / den
    coef_im = (ni * lr - nr * li) / den
    btr, bti = btr_ref[0], bti_ref[0]
    bbr = coef_re * btr - coef_im * bti
    bbi = coef_re * bti + coef_im * btr
    cr, ci = cr_ref[0], ci_ref[0]
    f_re = jnp.concatenate([cr * bbr[i:i + 1] - ci * bbi[i:i + 1] for i in range(SSM_GROUP)], axis=0)
    f_im = jnp.concatenate([cr * bbi[i:i + 1] + ci * bbr[i:i + 1] for i in range(SSM_GROUP)], axis=0)

    lrc = jnp.minimum(lrc_ref[0], LAM_RE_MAX)
    lic = lic_ref[0]
    lag = lax.broadcasted_iota(jnp.int32, (1, ch), 1).astype(F32)
    pl_re, pl_im = power(lrc, lic, lag)
    hi = lax.Precision.HIGHEST
    klag_ref[0] = (jnp.dot(f_re, pl_re, precision=hi, preferred_element_type=F32)
                   - jnp.dot(f_im, pl_im, precision=hi, preferred_element_type=F32))

    back = (ch - 1) - lax.broadcasted_iota(jnp.int32, (ch, 1), 0).astype(F32)
    q_re, q_im = power(lr, li, back)
    for i in range(SSM_GROUP):
        w_re = q_re * bbr[i:i + 1] - q_im * bbi[i:i + 1]
        w_im = q_re * bbi[i:i + 1] + q_im * bbr[i:i + 1]
        win_ref[0, i * ch:(i + 1) * ch, :] = jnp.concatenate([w_re, w_im], axis=1).astype(BF16)

    pw_re, pw_im = power(lrc, lic, lag + 1.0)
    ctr, cti = ctr_ref[0], cti_ref[0]
    p = SSM_STATE
    for o in range(SSM_GROUP):
        c_re, c_im = ctr[:, o:o + 1], cti[:, o:o + 1]
        wout_ref[0, :p, o * ch:(o + 1) * ch] = (c_re * pw_re - c_im * pw_im).astype(BF16)
        wout_ref[0, p:, o * ch:(o + 1) * ch] = (-(c_re * pw_im + c_im * pw_re)).astype(BF16)

    apr_ref[0], api_ref[0] = power(lr, li, float(ch))


def _ssm_tables(lam_re, lam_im, log_dt, b_re, b_im, c_re, c_im):
    n = lam_re.shape[0]
    p, g, ch = SSM_STATE, SSM_GROUP, SSM_CHUNK
    args = [lam_re.reshape(n, 1, p), lam_im.reshape(n, 1, p),
            lam_re.reshape(n, p, 1), lam_im.reshape(n, p, 1), log_dt.reshape(n, 1, 1),
            jnp.swapaxes(b_re, 1, 2), jnp.swapaxes(b_im, 1, 2), c_re, c_im,
            jnp.swapaxes(c_re, 1, 2), jnp.swapaxes(c_im, 1, 2)]
    spec = lambda a: pl.BlockSpec((1,) + a.shape[1:], lambda i: (i, 0, 0))
    out_shape = [jax.ShapeDtypeStruct((n, g * g, ch), F32),
                 jax.ShapeDtypeStruct((n, g * ch, 2 * p), BF16),
                 jax.ShapeDtypeStruct((n, 2 * p, g * ch), BF16),
                 jax.ShapeDtypeStruct((n, 1, p), F32),
                 jax.ShapeDtypeStruct((n, 1, p), F32)]
    return pl.pallas_call(
        _ssm_tables_kernel,
        grid=(n,),
        in_specs=[spec(a) for a in args],
        out_specs=[spec(s) for s in out_shape],
        out_shape=out_shape,
        compiler_params=_cparams(("arbitrary",)),
        name="ssm_tables",
    )(*args)


def _ssm_kernel(u_ref, klag_ref, win_ref, wout_ref, apr_ref, api_ref, d_ref, s_ref,
                mt_ref, hin_re, hin_im, hst_re, hst_im, *, batch):
    ch, g, p = SSM_CHUNK, SSM_GROUP, SSM_STATE
    nc = u_ref.shape[0]
    ncb = nc // batch
    causal = (lax.broadcasted_iota(jnp.int32, (ch, ch), 1)
              >= lax.broadcasted_iota(jnp.int32, (ch, ch), 0))

    def build(i, _):
        for o in range(g):
            kv = klag_ref[0, pl.ds(i * g + o, 1), :]
            tz = pltpu.roll(jnp.broadcast_to(kv, (ch, ch)), 0, 1, stride=1, stride_axis=0)
            mt_ref[pl.ds(pl.multiple_of(i * ch, ch), ch), o * ch:(o + 1) * ch] = (
                jnp.where(causal, tz, 0.0).astype(BF16))
        return 0

    lax.fori_loop(0, g, build, 0)

    u2 = u_ref.reshape(nc * g, LANES)
    s2 = s_ref.reshape(nc * g, LANES)
    lhs = jnp.concatenate([u2[pl.ds(i, nc, stride=g), :].astype(BF16) for i in range(g)], axis=1)
    y = _mm(lhs, mt_ref[...])
    hin = _mm(lhs, win_ref[0])
    hin_re[...] = hin[:, :p]
    hin_im[...] = hin[:, p:]
    a_re, a_im = apr_ref[0], api_ref[0]

    def scan(c, carry):
        new = []
        for b in range(batch):
            h_re, h_im = carry[b]
            r = b * ncb + c
            hst_re[pl.ds(r, 1), :] = h_re
            hst_im[pl.ds(r, 1), :] = h_im
            x_re = hin_re[pl.ds(r, 1), :]
            x_im = hin_im[pl.ds(r, 1), :]
            new.append((a_re * h_re - a_im * h_im + x_re, a_re * h_im + a_im * h_re + x_im))
        return tuple(new)

    zero = jnp.zeros((1, p), F32)
    lax.fori_loop(0, ncb, scan, tuple((zero, zero) for _ in range(batch)))
    y = y + _mm(hst_re[...].astype(BF16), wout_ref[0, :p, :]) + _mm(hst_im[...].astype(BF16), wout_ref[0, p:, :])
    for o in range(g):
        yo = y[:, o * ch:(o + 1) * ch] + u2[pl.ds(o, nc, stride=g), :] * d_ref[0, o:o + 1, :]
        s2[pl.ds(o, nc, stride=g), :] = jax.nn.gelu(yo)


def _ssm(ut, klag, win, wout, apr, api, d_skip, layer, batch):
    nc = ut.shape[0]
    g, ch, p = SSM_GROUP, SSM_CHUNK, SSM_STATE
    base = layer * SSM_GROUPS
    tab = lambda a: pl.BlockSpec((1,) + a.shape[1:], lambda i: (base + i, 0, 0))
    kern = functools.partial(_ssm_kernel, batch=batch)
    return pl.pallas_call(
        kern,
        grid=(SSM_GROUPS,),
        in_specs=[pl.BlockSpec((nc, g, LANES), lambda i: (0, i, 0)),
                  tab(klag), tab(win), tab(wout), tab(apr), tab(api),
                  pl.BlockSpec((1, g, 1), lambda i: (i, 0, 0))],
        out_specs=pl.BlockSpec((nc, g, LANES), lambda i: (0, i, 0)),
        out_shape=jax.ShapeDtypeStruct(ut.shape, F32),
        scratch_shapes=[pltpu.VMEM((g * ch, g * ch), BF16),
                        pltpu.VMEM((nc, p), F32), pltpu.VMEM((nc, p), F32),
                        pltpu.VMEM((nc, p), F32), pltpu.VMEM((nc, p), F32)],
        compiler_params=_cparams(("arbitrary",)),
        name="ssm",
    )(ut, klag, win, wout, apr, api, d_skip.reshape(SSM_GROUPS, g, 1))


def _merge_kernel(yt_ref, st_ref, zc_ref, halo_ref, g_ref, x_ref,
                  woa_ref, wglut_ref, bglu_ref, wos_ref, cw_ref, cb_ref, lng_ref, lnb_ref,
                  woc_ref, wout_ref, nf_ref, *rest, tiles_per_seq, with_router):
    if with_router:
        wrh_ref, wrl_ref, xn_ref, h2_ref, route_ref = rest
    else:
        xn_ref, h2_ref = rest
    tm = x_ref.shape[0]
    nch = tm // LANES
    i = pl.program_id(0)

    yt = jnp.concatenate([yt_ref[c] for c in range(nch)], axis=1)
    y_attn = _mm_tn(yt, woa_ref[...])

    st = jnp.concatenate([st_ref[c] for c in range(nch)], axis=1)
    gate = jax.nn.sigmoid(_mm(wglut_ref[...], st.astype(BF16)) + bglu_ref[...])
    y_ssm = _mm_tn((st * gate).astype(BF16), wos_ref[...])

    first = (i % tiles_per_seq) == 0
    zs = jnp.concatenate([jnp.where(first, 0.0, halo_ref[...]), zc_ref[...]], axis=0)
    nz = CONV_HALO + tm
    off = CONV_HALO - (CONV_K - 1)
    conv = cb_ref[...]
    for res in range(SUBLANES):
        zr = zs if res == 0 else pltpu.roll(zs, nz - res, 0)
        for kk in range(CONV_K):
            if (off + kk) % SUBLANES == res:
                lo = off + kk - res
                conv = conv + zr[lo:lo + tm, :] * cw_ref[kk:kk + 1, :]
    mu = jnp.mean(conv, axis=-1, keepdims=True)
    cen = conv - mu
    var = jnp.mean(cen * cen, axis=-1, keepdims=True)
    yc = jax.nn.silu(cen * lax.rsqrt(var + LN_EPS) * lng_ref[...] + lnb_ref[...])
    y_conv = _mm(yc.astype(BF16), woc_ref[...])

    d = D_MODEL
    merged = (g_ref[:, :d].astype(F32) * y_attn + g_ref[:, d:2 * d].astype(F32) * y_ssm
              + g_ref[:, 2 * d:].astype(F32) * y_conv)
    xn = x_ref[...] + _mm(merged.astype(BF16), wout_ref[...])
    xn_ref[...] = xn
    h = _rms(xn, nf_ref[...])
    h2_ref[...] = h.astype(h2_ref.dtype)

    if with_router:
        h_hi = h.astype(BF16)
        h_lo = (h - h_hi.astype(F32)).astype(BF16)
        logits = _mm(h_hi, wrh_ref[...]) + _mm(h_lo, wrh_ref[...]) + _mm(h_hi, wrl_ref[...])
        lane = lax.broadcasted_iota(jnp.int32, (tm, LANES), 1)
        logits = jnp.where(lane < N_EXPERTS, logits, NEG_BIG)
        m1 = jnp.max(logits, axis=-1, keepdims=True)
        i1 = jnp.min(jnp.where(logits == m1, lane, LANES), axis=-1, keepdims=True)
        rest_l = jnp.where(lane == i1, NEG_BIG, logits)
        m2 = jnp.max(rest_l, axis=-1, keepdims=True)
        i2 = jnp.min(jnp.where(rest_l == m2, lane, LANES), axis=-1, keepdims=True)
        e = jnp.exp(m2 - m1)
        g1 = 1.0 / (1.0 + e)
        g2 = e / (1.0 + e)
        route_ref[...] = jnp.where(lane == 0, i1.astype(F32),
                         jnp.where(lane == 1, i2.astype(F32),
                         jnp.where(lane == 2, g1, jnp.where(lane == 3, g2, 0.0))))


def _merge(yt, st, zc, gates, x, w, seq, with_router):
    t = x.shape[0]
    tm = min(TOKEN_TILE, t, seq)
    nch = tm // LANES
    hpt = tm // CONV_HALO
    row = lambda c: pl.BlockSpec((tm, c), lambda i: (i, 0))
    chunked = lambda c: pl.BlockSpec((nch, c, LANES), lambda i: (i, 0, 0))
    weights = [w["woa"], w["wglut"], w["bglu"], w["wos"], w["cw"], w["cb"], w["lng"], w["lnb"],
               w["woc"], w["wout"], w["nf"]]
    out_shape = [jax.ShapeDtypeStruct((t, D_MODEL), F32),
                 jax.ShapeDtypeStruct((t, D_MODEL), F32 if with_router else BF16)]
    out_specs = [row(D_MODEL), row(D_MODEL)]
    if with_router:
        weights += [w["wrh"], w["wrl"]]
        out_shape.append(jax.ShapeDtypeStruct((t, LANES), F32))
        out_specs.append(row(LANES))
    kern = functools.partial(_merge_kernel, tiles_per_seq=seq // tm, with_router=with_router)
    return pl.pallas_call(
        kern,
        grid=(t // tm,),
        in_specs=[chunked(N_HEADS * V_HEAD), chunked(SSM_WIDTH), row(CONV_WIDTH),
                  pl.BlockSpec((CONV_HALO, CONV_WIDTH), lambda i: (jnp.maximum(i * hpt - 1, 0), 0)),
                  row(3 * D_MODEL), row(D_MODEL)] + [_const_spec(a.shape) for a in weights],
        out_specs=out_specs,
        out_shape=out_shape,
        compiler_params=_cparams(("arbitrary",)),
        name="merge_router" if with_router else "merge",
    )(yt, st, zc, zc, gates, x, *weights)


def _ffn_kernel(x_ref, h_ref, w1_ref, w3_ref, w2_ref, o_ref):
    h = h_ref[...]
    f = w1_ref.shape[1]
    acc = x_ref[...]
    for lo in range(0, f, FFN_F_TILE):
        hi = min(lo + FFN_F_TILE, f)
        a = _mm(h, w1_ref[:, lo:hi])
        b = _mm(h, w3_ref[:, lo:hi])
        acc = acc + _mm((jax.nn.silu(a) * b).astype(BF16), w2_ref[lo:hi, :])
    o_ref[...] = acc


def _ffn(x, h, w1, w3, w2):
    t = x.shape[0]
    tm = min(FFN_ROW_TILE, t)
    row = pl.BlockSpec((tm, D_MODEL), lambda i: (i, 0))
    return pl.pallas_call(
        _ffn_kernel,
        grid=(t // tm,),
        in_specs=[row, row, _const_spec(w1.shape), _const_spec(w3.shape), _const_spec(w2.shape)],
        out_specs=row,
        out_shape=jax.ShapeDtypeStruct((t, D_MODEL), F32),
        compiler_params=_cparams(("arbitrary",)),
        name="ffn",
    )(x, h, w1, w3, w2)


def _dispatch_kernel(dest_ref, h_ref, xs_in_ref, xs_ref, sem):
    del xs_in_ref
    tm = h_ref.shape[0]

    def row_copy(t, k):
        return pltpu.make_async_copy(h_ref.at[pl.ds(t, 1), :],
                                     xs_ref.at[pl.ds(dest_ref[0, 0, TOP_K * t + k], 1), :], sem)

    def issue(t, c):
        for k in range(TOP_K):
            row_copy(t, k).start(priority=k)
        return c

    def drain(t, c):
        for k in range(TOP_K):
            row_copy(t, k).wait()
        return c

    lax.fori_loop(0, tm, issue, 0, unroll=ROW_DMA_UNROLL)
    lax.fori_loop(0, tm, drain, 0, unroll=ROW_DMA_UNROLL)


def _dispatch(h, dest, rows):
    t = h.shape[0]
    tm = min(ROUTE_TILE, t)
    nt = t // tm
    xs0 = jnp.zeros((rows, D_MODEL), h.dtype)
    return pl.pallas_call(
        _dispatch_kernel,
        grid=(nt,),
        in_specs=[pl.BlockSpec((1, 1, TOP_K * tm), lambda i: (i, 0, 0), memory_space=pltpu.SMEM),
                  pl.BlockSpec((tm, D_MODEL), lambda i: (i, 0)),
                  pl.BlockSpec(memory_space=pl.ANY)],
        out_specs=pl.BlockSpec(memory_space=pl.ANY),
        out_shape=jax.ShapeDtypeStruct((rows, D_MODEL), h.dtype),
        scratch_shapes=[pltpu.SemaphoreType.DMA(())],
        input_output_aliases={2: 0},
        compiler_params=_cparams(("arbitrary",)),
        name="moe_dispatch",
    )(dest.reshape(nt, 1, TOP_K * tm), h, xs0)


def _experts_kernel(te_ref, nu_ref, x_ref, w1_ref, w3_ref, w2_ref, y_ref, act_ref):
    i = pl.program_id(0)
    nf, _, tf = act_ref.shape
    used = i < nu_ref[0]

    @pl.when(jnp.logical_not(used))
    def _():
        y_ref[...] = jnp.zeros_like(y_ref)

    @pl.when(used)
    def _():
        xb = x_ref[...].astype(BF16)
        for c in range(nf):
            a = _mm(xb, w1_ref[0, :, c * tf:(c + 1) * tf])
            b = _mm(xb, w3_ref[0, :, c * tf:(c + 1) * tf])
            act_ref[c] = (jax.nn.silu(a) * b).astype(BF16)
        act = jnp.concatenate([act_ref[c] for c in range(nf)], axis=1)
        y_ref[...] = _mm(act, w2_ref[0])


def _experts(xs, tile_expert, n_used, w1, w3, w2):
    rows = xs.shape[0]
    tm = MOE_ROW_TILE
    f = w1.shape[2]
    tf = min(MOE_F_TILE, f)
    nf = f // tf
    once = pl.Buffered(1)
    grid_spec = pltpu.PrefetchScalarGridSpec(
        num_scalar_prefetch=2,
        grid=(rows // tm,),
        in_specs=[pl.BlockSpec((tm, D_MODEL), lambda i, te, nu: (i, 0)),
                  pl.BlockSpec((1, D_MODEL, f), lambda i, te, nu: (te[i], 0, 0), pipeline_mode=once),
                  pl.BlockSpec((1, D_MODEL, f), lambda i, te, nu: (te[i], 0, 0), pipeline_mode=once),
                  pl.BlockSpec((1, f, D_MODEL), lambda i, te, nu: (te[i], 0, 0), pipeline_mode=once)],
        out_specs=pl.BlockSpec((tm, D_MODEL), lambda i, te, nu: (i, 0)),
        scratch_shapes=[pltpu.VMEM((nf, tm, tf), BF16)],
    )
    return pl.pallas_call(
        _experts_kernel,
        grid_spec=grid_spec,
        out_shape=jax.ShapeDtypeStruct((rows, D_MODEL), F32),
        compiler_params=_cparams(("arbitrary",)),
        name="moe_experts",
    )(tile_expert, n_used, xs, w1, w3, w2)


def _combine_kernel(dest_ref, x_ref, route_ref, nfin_ref, ys_ref, o_ref, buf_ref, sem):
    tm = x_ref.shape[0]

    def row_copy(t, k):
        return pltpu.make_async_copy(ys_ref.at[pl.ds(dest_ref[0, 0, TOP_K * t + k], 1), :],
                                     buf_ref.at[pl.ds(k * tm + t, 1), :], sem)

    def issue(t, c):
        for k in range(TOP_K):
            row_copy(t, k).start(priority=k)
        return c

    def drain(t, c):
        for k in range(TOP_K):
            row_copy(t, k).wait()
        return c

    lax.fori_loop(0, tm, issue, 0, unroll=ROW_DMA_UNROLL)
    lax.fori_loop(0, tm, drain, 0, unroll=ROW_DMA_UNROLL)
    route = route_ref[...]
    y = x_ref[...] + route[:, 2:3] * buf_ref[:tm, :] + route[:, 3:4] * buf_ref[tm:, :]
    o_ref[...] = _rms(y, nfin_ref[...])


def _combine(x, route, dest, ys, norm_final):
    t = x.shape[0]
    tm = min(ROUTE_TILE, t)
    nt = t // tm
    return pl.pallas_call(
        _combine_kernel,
        grid=(nt,),
        in_specs=[pl.BlockSpec((1, 1, TOP_K * tm), lambda i: (i, 0, 0), memory_space=pltpu.SMEM),
                  pl.BlockSpec((tm, D_MODEL), lambda i: (i, 0)),
                  pl.BlockSpec((tm, LANES), lambda i: (i, 0)),
                  _const_spec((1, D_MODEL)),
                  pl.BlockSpec(memory_space=pl.ANY)],
        out_specs=pl.BlockSpec((tm, D_MODEL), lambda i: (i, 0)),
        out_shape=jax.ShapeDtypeStruct((t, D_MODEL), F32),
        scratch_shapes=[pltpu.VMEM((TOP_K * tm, D_MODEL), F32), pltpu.SemaphoreType.DMA(())],
        compiler_params=_cparams(("arbitrary",)),
        name="moe_combine",
    )(dest.reshape(nt, 1, TOP_K * tm), x, route, norm_final.reshape(1, D_MODEL), ys)


def _moe_plan(route, row_tile):
    t = route.shape[0]
    m = t * TOP_K
    e_flat = route[:, :TOP_K].astype(jnp.int32).reshape(m)
    onehot = (e_flat[:, None] == jnp.arange(N_EXPERTS, dtype=jnp.int32)[None, :]).astype(jnp.int32)
    csum = jnp.cumsum(onehot, axis=0)
    rank = jnp.sum((csum - onehot) * onehot, axis=1)
    counts = csum[-1]
    padded = ((counts + row_tile - 1) // row_tile) * row_tile
    pad_ends = jnp.cumsum(padded)
    pad_starts = pad_ends - padded
    dest = jnp.sum(onehot * pad_starts[None, :], axis=1) + rank
    n_tiles = -(-m // row_tile) + N_EXPERTS
    tile_start = jnp.arange(n_tiles, dtype=jnp.int32) * row_tile
    tile_expert = jnp.minimum(jnp.sum((pad_ends[None, :] <= tile_start[:, None]).astype(jnp.int32), axis=1),
                              N_EXPERTS - 1)
    n_used = (pad_ends[-1] // row_tile).reshape(1)
    return dest.astype(jnp.int32), tile_expert.astype(jnp.int32), n_used.astype(jnp.int32), n_tiles * row_tile


def _prep_mixer_weights(l, w_in, q_norm, w_uq, kv_norm, w_ukv, w_o_attn, ssm_w_glu, ssm_b_glu, w_o_ssm,
                        conv_w, conv_b, conv_ln_g, conv_ln_b, w_o_conv, w_out, norm_ffn):
    o = IN_OFFSETS
    wi = w_in[l]
    low = wi[:, o[0]:o[3]]
    wa = jnp.pad(low, ((0, 0), (0, 4 * LANES - low.shape[1]))).astype(BF16)
    wq = jnp.pad(w_uq[l].reshape(Q_LORA, N_HEADS, QK_HEAD),
                 ((0, 0), (0, 0), (0, HEAD_PAD - QK_HEAD))).reshape(Q_LORA, N_HEADS * HEAD_PAD).astype(BF16)
    wkv = w_ukv[l].reshape(KV_LORA, N_HEADS, QK_NOPE + V_HEAD)
    wkn = jnp.pad(wkv[..., :QK_NOPE], ((0, 0), (0, 0), (0, HEAD_PAD - QK_NOPE))
                  ).reshape(KV_LORA, N_HEADS * HEAD_PAD).astype(BF16)
    wvt = wkv[..., QK_NOPE:].reshape(KV_LORA, N_HEADS * V_HEAD).T.astype(BF16)
    place = np.zeros((LANES, N_HEADS * HEAD_PAD), np.float32)
    for h in range(N_HEADS):
        for r in range(QK_ROPE):
            place[r, h * HEAD_PAD + QK_NOPE + r] = 1.0
    return dict(
        wa=wa, wut=wi[:, o[3]:o[4]].T.astype(BF16), wca=wi[:, o[4]:o[5]].astype(BF16),
        wcg=wi[:, o[5]:o[6]].astype(BF16), wg=wi[:, o[6]:o[7]].astype(BF16),
        qn=q_norm[l].reshape(1, Q_LORA), kvn=kv_norm[l].reshape(1, KV_LORA),
        wq=wq, wkn=wkn, wke=jnp.asarray(place, BF16), wvt=wvt,
        woa=w_o_attn[l].astype(BF16), wglut=ssm_w_glu[l].T.astype(BF16),
        bglu=ssm_b_glu[l].reshape(SSM_WIDTH, 1), wos=w_o_ssm[l].astype(BF16),
        cw=conv_w[l], cb=conv_b[l].reshape(1, CONV_WIDTH), lng=conv_ln_g[l].reshape(1, CONV_WIDTH),
        lnb=conv_ln_b[l].reshape(1, CONV_WIDTH), woc=w_o_conv[l].astype(BF16),
        wout=w_out[l].astype(BF16), nf=norm_ffn[l].reshape(1, D_MODEL))


def kernel(x, positions, norm_mix, w_in, q_norm, w_uq, kv_norm, w_ukv, w_o_attn, ssm_lam_re, ssm_lam_im, ssm_log_dt, ssm_b_re, ssm_b_im, ssm_c_re, ssm_c_im, ssm_d, ssm_w_glu, ssm_b_glu, w_o_ssm, conv_w, conv_b, conv_ln_g, conv_ln_b, w_o_conv, w_out, norm_ffn, ffn_w1, ffn_w3, ffn_w2, moe_router, moe_w1, moe_w3, moe_w2, norm_final):
    batch, seq, d = x.shape
    depth = w_in.shape[0]
    assert d == D_MODEL and depth == 2 and seq % SSM_CHUNK == 0
    t = batch * seq
    xf = x.reshape(t, d)
    cos_t, sin_t = _rope_tables(positions)
    ng = depth * SSM_GROUPS
    klag, win, wout_s, apr, api = _ssm_tables(
        ssm_lam_re.reshape(ng, SSM_STATE), ssm_lam_im.reshape(ng, SSM_STATE), ssm_log_dt.reshape(ng),
        ssm_b_re.reshape(ng, SSM_STATE, SSM_GROUP), ssm_b_im.reshape(ng, SSM_STATE, SSM_GROUP),
        ssm_c_re.reshape(ng, SSM_GROUP, SSM_STATE), ssm_c_im.reshape(ng, SSM_GROUP, SSM_STATE))

    for layer in range(depth):
        moe_layer = layer % 2 == 1
        w = _prep_mixer_weights(layer, w_in, q_norm, w_uq, kv_norm, w_ukv, w_o_attn, ssm_w_glu, ssm_b_glu,
                                w_o_ssm, conv_w, conv_b, conv_ln_g, conv_ln_b, w_o_conv, w_out, norm_ffn)
        q, k, vt, ut, zc, gates = _mixer_in(xf, norm_mix[layer].reshape(1, d), w, cos_t, sin_t)
        yt = _attention(q, k, vt, batch, seq)
        st = _ssm(ut, klag, win, wout_s, apr, api, ssm_d[layer], layer, batch)
        i = layer // 2
        if not moe_layer:
            xn, h2 = _merge(yt, st, zc, gates, xf, w, seq, with_router=False)
            xf = _ffn(xn, h2, ffn_w1[i].astype(BF16), ffn_w3[i].astype(BF16), ffn_w2[i].astype(BF16))
        else:
            wr = jnp.pad(moe_router[i], ((0, 0), (0, LANES - N_EXPERTS)))
            w["wrh"] = wr.astype(BF16)
            w["wrl"] = (wr - w["wrh"].astype(F32)).astype(BF16)
            xn, h2, route = _merge(yt, st, zc, gates, xf, w, seq, with_router=True)
            dest, tile_expert, n_used, rows = _moe_plan(route, MOE_ROW_TILE)
            xs = _dispatch(h2, dest, rows)
            ys = _experts(xs, tile_expert, n_used, moe_w1[i].astype(BF16), moe_w3[i].astype(BF16),
                          moe_w2[i].astype(BF16))
            xf = _combine(xn, route, dest, ys, norm_final)
    return xf.reshape(batch, seq, d)
```

```python
import functools

import numpy as np
import jax
import jax.numpy as jnp
from jax import lax
from jax.experimental import pallas as pl
from jax.experimental.pallas import tpu as pltpu

F32 = jnp.float32
BF16 = jnp.bfloat16

D_MODEL = 1024
N_HEADS = 8
Q_LORA = 256
KV_LORA = 128
QK_NOPE = 64
QK_ROPE = 32
QK_HEAD = QK_NOPE + QK_ROPE
V_HEAD = 64
ROPE_THETA = 10000.0
SSM_WIDTH = 512
SSM_GROUP = 16
SSM_GROUPS = SSM_WIDTH // SSM_GROUP
SSM_STATE = 64
SSM_CHUNK = 128
CONV_WIDTH = 512
CONV_K = 31
N_EXPERTS = 8
TOP_K = 2
RMS_EPS = 1e-6
LN_EPS = 1e-5
LAM_RE_MAX = -1e-4
IN_SPLITS = (Q_LORA, KV_LORA, QK_ROPE, SSM_WIDTH, CONV_WIDTH, CONV_WIDTH, 3 * D_MODEL)
IN_OFFSETS = tuple(int(v) for v in np.cumsum((0,) + IN_SPLITS))

LANES = 128
SUBLANES = 8
HEAD_PAD = LANES
VMEM_LIMIT_BYTES = 56 * 1024 * 1024

TOKEN_TILE = 512
ATTN_TILE = 1024
ATTN_COLS = 256
ATTN_ROWS = 256
ATTN_UNROLL = 4
MOE_ROW_TILE = 1024
MOE_F_TILE = 512
FFN_F_TILE = 512
FFN_ROW_TILE = 1024
ROUTE_TILE = 1024
ROPE_TILE = 2048
ROW_DMA_UNROLL = 8
CONV_HALO = 32
SUM_ROWS = 16
NEG_BIG = -1e30
LOG2_E = float(np.log2(np.e))


def _cparams(sem):
    return pltpu.CompilerParams(dimension_semantics=sem, vmem_limit_bytes=VMEM_LIMIT_BYTES)


def _const_spec(shape):
    nd = len(shape)
    return pl.BlockSpec(shape, lambda *_: (0,) * nd, pipeline_mode=pl.Buffered(1))


def _rms(x, g):
    return x * lax.rsqrt(jnp.mean(x * x, axis=-1, keepdims=True) + RMS_EPS) * g


def _mm(a, b):
    return jnp.dot(a, b, preferred_element_type=F32)


def _mm_nt(a, b):
    return lax.dot_general(a, b, (((1,), (1,)), ((), ())), preferred_element_type=F32)


def _mm_tn(a, b):
    return lax.dot_general(a, b, (((0,), (0,)), ((), ())), preferred_element_type=F32)


def _rope_tables_kernel(pos_ref, invf_ref, sign_ref, cos_ref, sin_ref):
    ang = pos_ref[...].astype(F32) * invf_ref[...]
    cos_ref[...] = jnp.cos(ang)
    sin_ref[...] = jnp.sin(ang) * sign_ref[...]


def _rope_tables(positions):
    t = positions.size
    tm = min(ROPE_TILE, t)
    half = QK_ROPE // 2
    inv_freq = ROPE_THETA ** (-np.arange(0, QK_ROPE, 2, dtype=np.float32) / QK_ROPE)
    invf = np.zeros((1, LANES), np.float32)
    sign = np.zeros((1, LANES), np.float32)
    invf[0, QK_NOPE:QK_NOPE + half] = inv_freq
    invf[0, QK_NOPE + half:QK_HEAD] = inv_freq
    sign[0, QK_NOPE:QK_NOPE + half] = -1.0
    sign[0, QK_NOPE + half:QK_HEAD] = 1.0
    return pl.pallas_call(
        _rope_tables_kernel,
        grid=(t // tm,),
        in_specs=[pl.BlockSpec((tm, 1), lambda i: (i, 0)),
                  pl.BlockSpec((1, LANES), lambda i: (0, 0)),
                  pl.BlockSpec((1, LANES), lambda i: (0, 0))],
        out_specs=[pl.BlockSpec((tm, LANES), lambda i: (i, 0))] * 2,
        out_shape=[jax.ShapeDtypeStruct((t, LANES), F32)] * 2,
        compiler_params=_cparams(("arbitrary",)),
        name="rope_tables",
    )(positions.reshape(t, 1), jnp.asarray(invf), jnp.asarray(sign))


def _rope_head(xh, cos, sin, lane):
    half = QK_ROPE // 2
    swapped = jnp.where(lane < QK_NOPE + half,
                        pltpu.roll(xh, LANES - half, 1), pltpu.roll(xh, half, 1))
    return xh * cos + swapped * sin


def _mixer_in_kernel(x_ref, nm_ref, wa_ref, wut_ref, wca_ref, wcg_ref, wg_ref, qn_ref, kvn_ref,
                     wq_ref, wkn_ref, wke_ref, wvt_ref, cos_ref, sin_ref,
                     q_ref, k_ref, vt_ref, ut_ref, zc_ref, g_ref):
    tm = x_ref.shape[0]
    hn = _rms(x_ref[...], nm_ref[...]).astype(BF16)
    za = _mm(hn, wa_ref[...])
    cqn = _rms(za[:, :Q_LORA], qn_ref[...]).astype(BF16)
    ckvn = _rms(za[:, Q_LORA:Q_LORA + KV_LORA], kvn_ref[...]).astype(BF16)
    kpe = za[:, Q_LORA + KV_LORA:].astype(BF16)
    q = _mm(cqn, wq_ref[...])
    k = _mm(ckvn, wkn_ref[...]) + _mm(kpe, wke_ref[...])
    cos = cos_ref[...]
    sin = sin_ref[...]
    lane = lax.broadcasted_iota(jnp.int32, (tm, LANES), 1)
    scale = QK_HEAD ** -0.5 * LOG2_E
    for h in range(N_HEADS):
        sl = slice(h * HEAD_PAD, (h + 1) * HEAD_PAD)
        q_ref[:, sl] = (_rope_head(q[:, sl], cos, sin, lane) * scale).astype(BF16)
        k_ref[:, sl] = _rope_head(k[:, sl], cos, sin, lane).astype(BF16)
    vt = _mm_nt(wvt_ref[...], ckvn)
    ut = _mm_nt(wut_ref[...], hn)
    for c in range(tm // LANES):
        sl = slice(c * LANES, (c + 1) * LANES)
        vt_ref[c] = vt[:, sl].astype(BF16)
        ut_ref[c] = ut[:, sl]
    zc_ref[...] = _mm(hn, wca_ref[...]) * jax.nn.sigmoid(_mm(hn, wcg_ref[...]))
    for c in range(3):
        sl = slice(c * D_MODEL, (c + 1) * D_MODEL)
        g_ref[:, sl] = jax.nn.sigmoid(_mm(hn, wg_ref[:, sl])).astype(BF16)


def _mixer_in(x, nm, w, cos_t, sin_t):
    t = x.shape[0]
    tm = min(TOKEN_TILE, t)
    nch = tm // LANES
    row = lambda c: pl.BlockSpec((tm, c), lambda i: (i, 0))
    chunked = lambda c: pl.BlockSpec((nch, c, LANES), lambda i: (i, 0, 0))
    weights = [nm, w["wa"], w["wut"], w["wca"], w["wcg"], w["wg"], w["qn"], w["kvn"],
               w["wq"], w["wkn"], w["wke"], w["wvt"]]
    return pl.pallas_call(
        _mixer_in_kernel,
        grid=(t // tm,),
        in_specs=[row(D_MODEL)] + [_const_spec(a.shape) for a in weights] + [row(LANES), row(LANES)],
        out_specs=[row(N_HEADS * HEAD_PAD), row(N_HEADS * HEAD_PAD), chunked(N_HEADS * V_HEAD),
                   chunked(SSM_WIDTH), row(CONV_WIDTH), row(3 * D_MODEL)],
        out_shape=[jax.ShapeDtypeStruct((t, N_HEADS * HEAD_PAD), BF16),
                   jax.ShapeDtypeStruct((t, N_HEADS * HEAD_PAD), BF16),
                   jax.ShapeDtypeStruct((t // LANES, N_HEADS * V_HEAD, LANES), BF16),
                   jax.ShapeDtypeStruct((t // LANES, SSM_WIDTH, LANES), F32),
                   jax.ShapeDtypeStruct((t, CONV_WIDTH), F32),
                   jax.ShapeDtypeStruct((t, 3 * D_MODEL), BF16)],
        compiler_params=_cparams(("arbitrary",)),
        name="mixer_in",
    )(x, *weights, cos_t, sin_t)


def _attn_kernel(q_ref, k_ref, vt_ref, o_ref, sa_ref, sb_ref, mxa_ref, mxb_ref, m_ref, acc_ref, *, tq, tk):
    qi = pl.program_id(2)
    cw = ATTN_COLS
    ncol = tq // cw
    nkc = tk // LANES
    rw = min(ATTN_ROWS, tk)
    ones_rows = (lax.broadcasted_iota(jnp.int32, (SUM_ROWS, rw), 0) == 0).astype(BF16)
    ping = (sa_ref, mxa_ref)
    pong = (sb_ref, mxb_ref)

    def score_pass(j, c, dst, masked):
        s_ref, mx_ref = dst
        qc = q_ref[pl.ds(pl.multiple_of(c * cw, cw), cw), :]
        mx = None
        for r in range(tk // rw):
            if masked and r * rw >= (c + 1) * cw:
                continue
            kb = k_ref[pl.ds(pl.multiple_of(j * tk + r * rw, rw), rw), :]
            st = _mm_nt(kb, qc)
            if masked and (r + 1) * rw > c * cw + 1:
                kpos = r * rw + lax.broadcasted_iota(jnp.int32, (rw, cw), 0)
                qpos = c * cw + lax.broadcasted_iota(jnp.int32, (rw, cw), 1)
                st = jnp.where(kpos <= qpos, st, NEG_BIG)
            s_ref[c, r * rw:(r + 1) * rw, :] = st
            mr = jnp.max(st, axis=0, keepdims=True)
            mx = mr if mx is None else jnp.maximum(mx, mr)
        mx_ref[c] = mx

    def value_pass(j, c, src, diag=False):
        s_ref, mx_ref = src
        m_old = m_ref[c]
        m_new = jnp.maximum(m_old, mx_ref[c])
        pv = None
        for r in range(tk // rw):
            if diag and r * rw >= (c + 1) * cw:
                continue
            p = jnp.exp2(s_ref[c, r * rw:(r + 1) * rw, :] - m_new).astype(BF16)
            vb = jnp.concatenate([vt_ref[j * nkc + r * (rw // LANES) + i] for i in range(rw // LANES)], axis=1)
            vb = jnp.concatenate([vb, ones_rows], axis=0)
            part = _mm(vb, p)
            pv = part if pv is None else pv + part
        acc_ref[c] = jnp.exp2(m_old - m_new) * acc_ref[c] + pv
        m_ref[c] = m_new

    def over_cols(fn, static=False):
        if static:
            for c in range(ncol):
                fn(c)
            return

        def body(c, carry):
            fn(c)
            return carry
        lax.fori_loop(0, ncol, body, 0, unroll=ATTN_UNROLL)

    def step(j, src, dst, masked):
        def fn(c):
            value_pass(j, c, src)
            score_pass(j + 1, c, dst, masked)
        over_cols(fn, static=masked)

    m_ref[...] = jnp.full(m_ref.shape, NEG_BIG, F32)
    acc_ref[...] = jnp.zeros(acc_ref.shape, F32)

    @pl.when(qi == 0)
    def _():
        over_cols(lambda c: score_pass(0, c, ping, True), static=True)

    @pl.when(qi > 0)
    def _():
        over_cols(lambda c: score_pass(0, c, ping, False))

    n_plain = jnp.maximum(qi - 1, 0)

    def pair(i, carry):
        step(2 * i, ping, pong, False)
        step(2 * i + 1, pong, ping, False)
        return carry

    lax.fori_loop(0, n_plain // 2, pair, 0)

    @pl.when(n_plain % 2 == 1)
    def _():
        step(n_plain - 1, ping, pong, False)

    @pl.when(jnp.logical_and(qi > 0, qi % 2 == 1))
    def _():
        step(qi - 1, ping, pong, True)
        over_cols(lambda c: value_pass(qi, c, pong, diag=True), static=True)

    @pl.when(jnp.logical_and(qi > 0, qi % 2 == 0))
    def _():
        step(qi - 1, pong, ping, True)

    @pl.when(qi % 2 == 0)
    def _():
        over_cols(lambda c: value_pass(qi, c, ping, diag=True), static=True)

    for c in range(ncol):
        acc = acc_ref[c]
        out = acc[:V_HEAD] / acc[V_HEAD:V_HEAD + 1]
        for i in range(cw // LANES):
            o_ref[c * (cw // LANES) + i] = out[:, i * LANES:(i + 1) * LANES].astype(BF16)


def _attention(q, k, vt, batch, seq):
    tq = tk = min(ATTN_TILE, seq)
    nq = seq // tq
    ncol = tq // ATTN_COLS
    kern = functools.partial(_attn_kernel, tq=tq, tk=tk)
    return pl.pallas_call(
        kern,
        grid=(batch, N_HEADS, nq),
        in_specs=[pl.BlockSpec((tq, HEAD_PAD), lambda b, h, i: (b * nq + i, h)),
                  pl.BlockSpec((seq, HEAD_PAD), lambda b, h, i: (b, h)),
                  pl.BlockSpec((seq // LANES, V_HEAD, LANES), lambda b, h, i: (b, h, 0))],
        out_specs=pl.BlockSpec((tq // LANES, V_HEAD, LANES), lambda b, h, i: (b * nq + i, h, 0)),
        out_shape=jax.ShapeDtypeStruct((batch * seq // LANES, N_HEADS * V_HEAD, LANES), BF16),
        scratch_shapes=[pltpu.VMEM((ncol, tk, ATTN_COLS), F32),
                        pltpu.VMEM((ncol, tk, ATTN_COLS), F32),
                        pltpu.VMEM((ncol, 1, ATTN_COLS), F32),
                        pltpu.VMEM((ncol, 1, ATTN_COLS), F32),
                        pltpu.VMEM((ncol, 1, ATTN_COLS), F32),
                        pltpu.VMEM((ncol, V_HEAD + SUM_ROWS, ATTN_COLS), F32)],
        compiler_params=_cparams(("arbitrary", "arbitrary", "arbitrary")),
        name="attention",
    )(q, k, vt)


def _ssm_tables_kernel(lrr_ref, lir_ref, lrc_ref, lic_ref, ldt_ref, btr_ref, bti_ref,
                       cr_ref, ci_ref, ctr_ref, cti_ref,
                       klag_ref, win_ref, wout_ref, apr_ref, api_ref):
    ch = SSM_CHUNK
    dt = jnp.exp(ldt_ref[0])

    def power(lr, li, n):
        mag = jnp.exp(lr * dt * n)
        return mag * jnp.cos(li * dt * n), mag * jnp.sin(li * dt * n)

    lr = jnp.minimum(lrr_ref[0], LAM_RE_MAX)
    li = lir_ref[0]
    a_re, a_im = power(lr, li, 1.0)
    den = lr * lr + li * li
    nr, ni = a_re - 1.0, a_im
    coef_re = (nr * lr + ni * li) / den
    coef_im = (ni * lr - nr * li) / den
    btr, bti = btr_ref[0], bti_ref[0]
    bbr = coef_re * btr - coef_im * bti
    bbi = coef_re * bti + coef_im * btr
    cr, ci = cr_ref[0], ci_ref[0]
    f_re = jnp.concatenate([cr * bbr[i:i + 1] - ci * bbi[i:i + 1] for i in range(SSM_GROUP)], axis=0)
    f_im = jnp.concatenate([cr * bbi[i:i + 1] + ci * bbr[i:i + 1] for i in range(SSM_GROUP)], axis=0)

    lrc = jnp.minimum(lrc_ref[0], LAM_RE_MAX)
    lic = lic_ref[0]
    lag = lax.broadcasted_iota(jnp.int32, (1, ch), 1).astype(F32)
    pl_re, pl_im = power(lrc, lic, lag)
    hi = lax.Precision.HIGHEST
    klag_ref[0] = (jnp.dot(f_re, pl_re, precision=hi, preferred_element_type=F32)
                   - jnp.dot(f_im, pl_im, precision=hi, preferred_element_type=F32))

    back = (ch - 1) - lax.broadcasted_iota(jnp.int32, (ch, 1), 0).astype(F32)
    q_re, q_im = power(lr, li, back)
    for i in range(SSM_GROUP):
        w_re = q_re * bbr[i:i + 1] - q_im * bbi[i:i + 1]
        w_im = q_re * bbi[i:i + 1] + q_im * bbr[i:i + 1]
        win_ref[0, i * ch:(i + 1) * ch, :] = jnp.concatenate([w_re, w_im], axis=1).astype(BF16)

    pw_re, pw_im = power(lrc, lic, lag + 1.0)
    ctr, cti = ctr_ref[0], cti_ref[0]
    p = SSM_STATE
    for o in range(SSM_GROUP):
        c_re, c_im = ctr[:, o:o + 1], cti[:, o:o + 1]
        wout_ref[0, :p, o * ch:(o + 1) * ch] = (c_re * pw_re - c_im * pw_im).astype(BF16)
        wout_ref[0, p:, o * ch:(o + 1) * ch] = (-(c_re * pw_im + c_im * pw_re)).astype(BF16)

    apr_ref[0], api_ref[0] = power(lr, li, float(ch))


def _ssm_tables(lam_re, lam_im, log_dt, b_re, b_im, c_re, c_im):
    n = lam_re.shape[0]
    p, g, ch = SSM_STATE, SSM_GROUP, SSM_CHUNK
    args = [lam_re.reshape(n, 1, p), lam_im.reshape(n, 1, p),
            lam_re.reshape(n, p, 1), lam_im.reshape(n, p, 1), log_dt.reshape(n, 1, 1),
            jnp.swapaxes(b_re, 1, 2), jnp.swapaxes(b_im, 1, 2), c_re, c_im,
            jnp.swapaxes(c_re, 1, 2), jnp.swapaxes(c_im, 1, 2)]
    spec = lambda a: pl.BlockSpec((1,) + a.shape[1:], lambda i: (i, 0, 0))
    out_shape = [jax.ShapeDtypeStruct((n, g * g, ch), F32),
                 jax.ShapeDtypeStruct((n, g * ch, 2 * p), BF16),
                 jax.ShapeDtypeStruct((n, 2 * p, g * ch), BF16),
                 jax.ShapeDtypeStruct((n, 1, p), F32),
                 jax.ShapeDtypeStruct((n, 1, p), F32)]
    return pl.pallas_call(
        _ssm_tables_kernel,
        grid=(n,),
        in_specs=[spec(a) for a in args],
        out_specs=[spec(s) for s in out_shape],
        out_shape=out_shape,
        compiler_params=_cparams(("arbitrary",)),
        name="ssm_tables",
    )(*args)


def _ssm_kernel(u_ref, klag_ref, win_ref, wout_ref, apr_ref, api_ref, d_ref, s_ref,
                mt_ref, hin_re, hin_im, hst_re, hst_im, *, batch):
    ch, g, p = SSM_CHUNK, SSM_GROUP, SSM_STATE
    nc = u_ref.shape[0]
    ncb = nc // batch
    causal = (lax.broadcasted_iota(jnp.int32, (ch, ch), 1)
              >= lax.broadcasted_iota(jnp.int32, (ch, ch), 0))

    def build(i, _):
        for o in range(g):
            kv = klag_ref[0, pl.ds(i * g + o, 1), :]
            tz = pltpu.roll(jnp.broadcast_to(kv, (ch, ch)), 0, 1, stride=1, stride_axis=0)
            mt_ref[pl.ds(pl.multiple_of(i * ch, ch), ch), o * ch:(o + 1) * ch] = (
                jnp.where(causal, tz, 0.0).astype(BF16))
        return 0

    lax.fori_loop(0, g, build, 0, unroll=True)

    u2 = u_ref.reshape(nc * g, LANES)
    s2 = s_ref.reshape(nc * g, LANES)
    lhs = jnp.concatenate([u2[pl.ds(i, nc, stride=g), :].astype(BF16) for i in range(g)], axis=1)
    y = _mm(lhs, mt_ref[...])
    hin = _mm(lhs, win_ref[0])
    hin_re[...] = hin[:, :p]
    hin_im[...] = hin[:, p:]
    a_re, a_im = apr_ref[0], api_ref[0]

    def scan(c, carry):
        new = []
        for b in range(batch):
            h_re, h_im = carry[b]
            r = b * ncb + c
            hst_re[pl.ds(r, 1), :] = h_re
            hst_im[pl.ds(r, 1), :] = h_im
            x_re = hin_re[pl.ds(r, 1), :]
            x_im = hin_im[pl.ds(r, 1), :]
            new.append((a_re * h_re - a_im * h_im + x_re, a_re * h_im + a_im * h_re + x_im))
        return tuple(new)

    zero = jnp.zeros((1, p), F32)
    lax.fori_loop(0, ncb, scan, tuple((zero, zero) for _ in range(batch)))
    y = y + _mm(hst_re[...].astype(BF16), wout_ref[0, :p, :]) + _mm(hst_im[...].astype(BF16), wout_ref[0, p:, :])
    for o in range(g):
        yo = y[:, o * ch:(o + 1) * ch] + u2[pl.ds(o, nc, stride=g), :] * d_ref[0, o:o + 1, :]
        s2[pl.ds(o, nc, stride=g), :] = jax.nn.gelu(yo)


def _ssm(ut, klag, win, wout, apr, api, d_skip, layer, batch):
    nc = ut.shape[0]
    g, ch, p = SSM_GROUP, SSM_CHUNK, SSM_STATE
    base = layer * SSM_GROUPS
    tab = lambda a: pl.BlockSpec((1,) + a.shape[1:], lambda i: (base + i, 0, 0))
    kern = functools.partial(_ssm_kernel, batch=batch)
    return pl.pallas_call(
        kern,
        grid=(SSM_GROUPS,),
        in_specs=[pl.BlockSpec((nc, g, LANES), lambda i: (0, i, 0)),
                  tab(klag), tab(win), tab(wout), tab(apr), tab(api),
                  pl.BlockSpec((1, g, 1), lambda i: (i, 0, 0))],
        out_specs=pl.BlockSpec((nc, g, LANES), lambda i: (0, i, 0)),
        out_shape=jax.ShapeDtypeStruct(ut.shape, F32),
        scratch_shapes=[pltpu.VMEM((g * ch, g * ch), BF16),
                        pltpu.VMEM((nc, p), F32), pltpu.VMEM((nc, p), F32),
                        pltpu.VMEM((nc, p), F32), pltpu.VMEM((nc, p), F32)],
        compiler_params=_cparams(("arbitrary",)),
        name="ssm",
    )(ut, klag, win, wout, apr, api, d_skip.reshape(SSM_GROUPS, g, 1))


def _merge_kernel(yt_ref, st_ref, zc_ref, halo_ref, g_ref, x_ref,
                  woa_ref, wglut_ref, bglu_ref, wos_ref, cw_ref, cb_ref, lng_ref, lnb_ref,
                  woc_ref, wout_ref, nf_ref, *rest, tiles_per_seq, with_router):
    if with_router:
        wrh_ref, wrl_ref, xn_ref, h2_ref, route_ref = rest
    else:
        xn_ref, h2_ref = rest
    tm = x_ref.shape[0]
    nch = tm // LANES
    i = pl.program_id(0)

    yt = jnp.concatenate([yt_ref[c] for c in range(nch)], axis=1)
    y_attn = _mm_tn(yt, woa_ref[...])

    st = jnp.concatenate([st_ref[c] for c in range(nch)], axis=1)
    gate = jax.nn.sigmoid(_mm(wglut_ref[...], st.astype(BF16)) + bglu_ref[...])
    y_ssm = _mm_tn((st * gate).astype(BF16), wos_ref[...])

    first = (i % tiles_per_seq) == 0
    zs = jnp.concatenate([jnp.where(first, 0.0, halo_ref[...]), zc_ref[...]], axis=0)
    nz = CONV_HALO + tm
    off = CONV_HALO - (CONV_K - 1)
    conv = cb_ref[...]
    for res in range(SUBLANES):
        zr = zs if res == 0 else pltpu.roll(zs, nz - res, 0)
        for kk in range(CONV_K):
            if (off + kk) % SUBLANES == res:
                lo = off + kk - res
                conv = conv + zr[lo:lo + tm, :] * cw_ref[kk:kk + 1, :]
    mu = jnp.mean(conv, axis=-1, keepdims=True)
    cen = conv - mu
    var = jnp.mean(cen * cen, axis=-1, keepdims=True)
    yc = jax.nn.silu(cen * lax.rsqrt(var + LN_EPS) * lng_ref[...] + lnb_ref[...])
    y_conv = _mm(yc.astype(BF16), woc_ref[...])

    d = D_MODEL
    merged = (g_ref[:, :d].astype(F32) * y_attn + g_ref[:, d:2 * d].astype(F32) * y_ssm
              + g_ref[:, 2 * d:].astype(F32) * y_conv)
    xn = x_ref[...] + _mm(merged.astype(BF16), wout_ref[...])
    xn_ref[...] = xn
    h = _rms(xn, nf_ref[...])
    h2_ref[...] = h.astype(h2_ref.dtype)

    if with_router:
        h_hi = h.astype(BF16)
        h_lo = (h - h_hi.astype(F32)).astype(BF16)
        logits = _mm(h_hi, wrh_ref[...]) + _mm(h_lo, wrh_ref[...]) + _mm(h_hi, wrl_ref[...])
        lane = lax.broadcasted_iota(jnp.int32, (tm, LANES), 1)
        logits = jnp.where(lane < N_EXPERTS, logits, NEG_BIG)
        m1 = jnp.max(logits, axis=-1, keepdims=True)
        i1 = jnp.min(jnp.where(logits == m1, lane, LANES), axis=-1, keepdims=True)
        rest_l = jnp.where(lane == i1, NEG_BIG, logits)
        m2 = jnp.max(rest_l, axis=-1, keepdims=True)
        i2 = jnp.min(jnp.where(rest_l == m2, lane, LANES), axis=-1, keepdims=True)
        e = jnp.exp(m2 - m1)
        g1 = 1.0 / (1.0 + e)
        g2 = e / (1.0 + e)
        route_ref[...] = jnp.where(lane == 0, i1.astype(F32),
                         jnp.where(lane == 1, i2.astype(F32),
                         jnp.where(lane == 2, g1, jnp.where(lane == 3, g2, 0.0))))


def _merge(yt, st, zc, gates, x, w, seq, with_router):
    t = x.shape[0]
    tm = min(TOKEN_TILE, t, seq)
    nch = tm // LANES
    hpt = tm // CONV_HALO
    row = lambda c: pl.BlockSpec((tm, c), lambda i: (i, 0))
    chunked = lambda c: pl.BlockSpec((nch, c, LANES), lambda i: (i, 0, 0))
    weights = [w["woa"], w["wglut"], w["bglu"], w["wos"], w["cw"], w["cb"], w["lng"], w["lnb"],
               w["woc"], w["wout"], w["nf"]]
    out_shape = [jax.ShapeDtypeStruct((t, D_MODEL), F32),
                 jax.ShapeDtypeStruct((t, D_MODEL), F32 if with_router else BF16)]
    out_specs = [row(D_MODEL), row(D_MODEL)]
    if with_router:
        weights += [w["wrh"], w["wrl"]]
        out_shape.append(jax.ShapeDtypeStruct((t, LANES), F32))
        out_specs.append(row(LANES))
    kern = functools.partial(_merge_kernel, tiles_per_seq=seq // tm, with_router=with_router)
    return pl.pallas_call(
        kern,
        grid=(t // tm,),
        in_specs=[chunked(N_HEADS * V_HEAD), chunked(SSM_WIDTH), row(CONV_WIDTH),
                  pl.BlockSpec((CONV_HALO, CONV_WIDTH), lambda i: (jnp.maximum(i * hpt - 1, 0), 0)),
                  row(3 * D_MODEL), row(D_MODEL)] + [_const_spec(a.shape) for a in weights],
        out_specs=out_specs,
        out_shape=out_shape,
        compiler_params=_cparams(("arbitrary",)),
        name="merge_router" if with_router else "merge",
    )(yt, st, zc, zc, gates, x, *weights)


def _ffn_kernel(x_ref, h_ref, w1_ref, w3_ref, w2_ref, o_ref):
    h = h_ref[...]
    f = w1_ref.shape[1]
    acc = x_ref[...]
    for lo in range(0, f, FFN_F_TILE):
        hi = min(lo + FFN_F_TILE, f)
        a = _mm(h, w1_ref[:, lo:hi])
        b = _mm(h, w3_ref[:, lo:hi])
        acc = acc + _mm((jax.nn.silu(a) * b).astype(BF16), w2_ref[lo:hi, :])
    o_ref[...] = acc


def _ffn(x, h, w1, w3, w2):
    t = x.shape[0]
    tm = min(FFN_ROW_TILE, t)
    row = pl.BlockSpec((tm, D_MODEL), lambda i: (i, 0))
    return pl.pallas_call(
        _ffn_kernel,
        grid=(t // tm,),
        in_specs=[row, row, _const_spec(w1.shape), _const_spec(w3.shape), _const_spec(w2.shape)],
        out_specs=row,
        out_shape=jax.ShapeDtypeStruct((t, D_MODEL), F32),
        compiler_params=_cparams(("arbitrary",)),
        name="ffn",
    )(x, h, w1, w3, w2)


def _dispatch_kernel(dest_ref, h_ref, xs_in_ref, xs_ref, sem):
    del xs_in_ref
    tm = h_ref.shape[0]

    def row_copy(t, k):
        return pltpu.make_async_copy(h_ref.at[pl.ds(t, 1), :],
                                     xs_ref.at[pl.ds(dest_ref[0, 0, TOP_K * t + k], 1), :], sem)

    def issue(t, c):
        for k in range(TOP_K):
            row_copy(t, k).start(priority=k)
        return c

    def drain(t, c):
        for k in range(TOP_K):
            row_copy(t, k).wait()
        return c

    lax.fori_loop(0, tm, issue, 0, unroll=ROW_DMA_UNROLL)
    lax.fori_loop(0, tm, drain, 0, unroll=ROW_DMA_UNROLL)


def _dispatch(h, dest, rows):
    t = h.shape[0]
    tm = min(ROUTE_TILE, t)
    nt = t // tm
    xs0 = jnp.zeros((rows, D_MODEL), h.dtype)
    return pl.pallas_call(
        _dispatch_kernel,
        grid=(nt,),
        in_specs=[pl.BlockSpec((1, 1, TOP_K * tm), lambda i: (i, 0, 0), memory_space=pltpu.SMEM),
                  pl.BlockSpec((tm, D_MODEL), lambda i: (i, 0)),
                  pl.BlockSpec(memory_space=pl.ANY)],
        out_specs=pl.BlockSpec(memory_space=pl.ANY),
        out_shape=jax.ShapeDtypeStruct((rows, D_MODEL), h.dtype),
        scratch_shapes=[pltpu.SemaphoreType.DMA(())],
        input_output_aliases={2: 0},
        compiler_params=_cparams(("arbitrary",)),
        name="moe_dispatch",
    )(dest.reshape(nt, 1, TOP_K * tm), h, xs0)


def _experts_kernel(te_ref, nu_ref, x_ref, w1_ref, w3_ref, w2_ref, y_ref, act_ref):
    i = pl.program_id(0)
    nf, _, tf = act_ref.shape
    used = i < nu_ref[0]

    @pl.when(jnp.logical_not(used))
    def _():
        y_ref[...] = jnp.zeros_like(y_ref)

    @pl.when(used)
    def _():
        xb = x_ref[...].astype(BF16)
        for c in range(nf):
            a = _mm(xb, w1_ref[0, :, c * tf:(c + 1) * tf])
            b = _mm(xb, w3_ref[0, :, c * tf:(c + 1) * tf])
            act_ref[c] = (jax.nn.silu(a) * b).astype(BF16)
        act = jnp.concatenate([act_ref[c] for c in range(nf)], axis=1)
        y_ref[...] = _mm(act, w2_ref[0])


def _experts(xs, tile_expert, n_used, w1, w3, w2):
    rows = xs.shape[0]
    tm = MOE_ROW_TILE
    f = w1.shape[2]
    tf = min(MOE_F_TILE, f)
    nf = f // tf
    once = pl.Buffered(1)
    grid_spec = pltpu.PrefetchScalarGridSpec(
        num_scalar_prefetch=2,
        grid=(rows // tm,),
        in_specs=[pl.BlockSpec((tm, D_MODEL), lambda i, te, nu: (i, 0)),
                  pl.BlockSpec((1, D_MODEL, f), lambda i, te, nu: (te[i], 0, 0), pipeline_mode=once),
                  pl.BlockSpec((1, D_MODEL, f), lambda i, te, nu: (te[i], 0, 0), pipeline_mode=once),
                  pl.BlockSpec((1, f, D_MODEL), lambda i, te, nu: (te[i], 0, 0), pipeline_mode=once)],
        out_specs=pl.BlockSpec((tm, D_MODEL), lambda i, te, nu: (i, 0)),
        scratch_shapes=[pltpu.VMEM((nf, tm, tf), BF16)],
    )
    return pl.pallas_call(
        _experts_kernel,
        grid_spec=grid_spec,
        out_shape=jax.ShapeDtypeStruct((rows, D_MODEL), F32),
        compiler_params=_cparams(("arbitrary",)),
        name="moe_experts",
    )(tile_expert, n_used, xs, w1, w3, w2)


def _combine_kernel(dest_ref, x_ref, route_ref, nfin_ref, ys_ref, o_ref, buf_ref, sem):
    tm = x_ref.shape[0]

    def row_copy(t, k):
        return pltpu.make_async_copy(ys_ref.at[pl.ds(dest_ref[0, 0, TOP_K * t + k], 1), :],
                                     buf_ref.at[pl.ds(k * tm + t, 1), :], sem)

    def issue(t, c):
        for k in range(TOP_K):
            row_copy(t, k).start(priority=k)
        return c

    def drain(t, c):
        for k in range(TOP_K):
            row_copy(t, k).wait()
        return c

    lax.fori_loop(0, tm, issue, 0, unroll=ROW_DMA_UNROLL)
    lax.fori_loop(0, tm, drain, 0, unroll=ROW_DMA_UNROLL)
    route = route_ref[...]
    y = x_ref[...] + route[:, 2:3] * buf_ref[:tm, :] + route[:, 3:4] * buf_ref[tm:, :]
    o_ref[...] = _rms(y, nfin_ref[...])


def _combine(x, route, dest, ys, norm_final):
    t = x.shape[0]
    tm = min(ROUTE_TILE, t)
    nt = t // tm
    return pl.pallas_call(
        _combine_kernel,
        grid=(nt,),
        in_specs=[pl.BlockSpec((1, 1, TOP_K * tm), lambda i: (i, 0, 0), memory_space=pltpu.SMEM),
                  pl.BlockSpec((tm, D_MODEL), lambda i: (i, 0)),
                  pl.BlockSpec((tm, LANES), lambda i: (i, 0)),
                  _const_spec((1, D_MODEL)),
                  pl.BlockSpec(memory_space=pl.ANY)],
        out_specs=pl.BlockSpec((tm, D_MODEL), lambda i: (i, 0)),
        out_shape=jax.ShapeDtypeStruct((t, D_MODEL), F32),
        scratch_shapes=[pltpu.VMEM((TOP_K * tm, D_MODEL), F32), pltpu.SemaphoreType.DMA(())],
        compiler_params=_cparams(("arbitrary",)),
        name="moe_combine",
    )(dest.reshape(nt, 1, TOP_K * tm), x, route, norm_final.reshape(1, D_MODEL), ys)


def _moe_plan(route, row_tile):
    t = route.shape[0]
    m = t * TOP_K
    e_flat = route[:, :TOP_K].astype(jnp.int32).reshape(m)
    onehot = (e_flat[:, None] == jnp.arange(N_EXPERTS, dtype=jnp.int32)[None, :]).astype(jnp.int32)
    csum = jnp.cumsum(onehot, axis=0)
    rank = jnp.sum((csum - onehot) * onehot, axis=1)
    counts = csum[-1]
    padded = ((counts + row_tile - 1) // row_tile) * row_tile
    pad_ends = jnp.cumsum(padded)
    pad_starts = pad_ends - padded
    dest = jnp.sum(onehot * pad_starts[None, :], axis=1) + rank
    n_tiles = -(-m // row_tile) + N_EXPERTS
    tile_start = jnp.arange(n_tiles, dtype=jnp.int32) * row_tile
    tile_expert = jnp.minimum(jnp.sum((pad_ends[None, :] <= tile_start[:, None]).astype(jnp.int32), axis=1),
                              N_EXPERTS - 1)
    n_used = (pad_ends[-1] // row_tile).reshape(1)
    return dest.astype(jnp.int32), tile_expert.astype(jnp.int32), n_used.astype(jnp.int32), n_tiles * row_tile


def _prep_mixer_weights(l, w_in, q_norm, w_uq, kv_norm, w_ukv, w_o_attn, ssm_w_glu, ssm_b_glu, w_o_ssm,
                        conv_w, conv_b, conv_ln_g, conv_ln_b, w_o_conv, w_out, norm_ffn):
    o = IN_OFFSETS
    wi = w_in[l]
    low = wi[:, o[0]:o[3]]
    wa = jnp.pad(low, ((0, 0), (0, 4 * LANES - low.shape[1]))).astype(BF16)
    wq = jnp.pad(w_uq[l].reshape(Q_LORA, N_HEADS, QK_HEAD),
                 ((0, 0), (0, 0), (0, HEAD_PAD - QK_HEAD))).reshape(Q_LORA, N_HEADS * HEAD_PAD).astype(BF16)
    wkv = w_ukv[l].reshape(KV_LORA, N_HEADS, QK_NOPE + V_HEAD)
    wkn = jnp.pad(wkv[..., :QK_NOPE], ((0, 0), (0, 0), (0, HEAD_PAD - QK_NOPE))
                  ).reshape(KV_LORA, N_HEADS * HEAD_PAD).astype(BF16)
    wvt = wkv[..., QK_NOPE:].reshape(KV_LORA, N_HEADS * V_HEAD).T.astype(BF16)
    place = np.zeros((LANES, N_HEADS * HEAD_PAD), np.float32)
    for h in range(N_HEADS):
        for r in range(QK_ROPE):
            place[r, h * HEAD_PAD + QK_NOPE + r] = 1.0
    return dict(
        wa=wa, wut=wi[:, o[3]:o[4]].T.astype(BF16), wca=wi[:, o[4]:o[5]].astype(BF16),
        wcg=wi[:, o[5]:o[6]].astype(BF16), wg=wi[:, o[6]:o[7]].astype(BF16),
        qn=q_norm[l].reshape(1, Q_LORA), kvn=kv_norm[l].reshape(1, KV_LORA),
        wq=wq, wkn=wkn, wke=jnp.asarray(place, BF16), wvt=wvt,
        woa=w_o_attn[l].astype(BF16), wglut=ssm_w_glu[l].T.astype(BF16),
        bglu=ssm_b_glu[l].reshape(SSM_WIDTH, 1), wos=w_o_ssm[l].astype(BF16),
        cw=conv_w[l], cb=conv_b[l].reshape(1, CONV_WIDTH), lng=conv_ln_g[l].reshape(1, CONV_WIDTH),
        lnb=conv_ln_b[l].reshape(1, CONV_WIDTH), woc=w_o_conv[l].astype(BF16),
        wout=w_out[l].astype(BF16), nf=norm_ffn[l].reshape(1, D_MODEL))


def kernel(x, positions, norm_mix, w_in, q_norm, w_uq, kv_norm, w_ukv, w_o_attn, ssm_lam_re, ssm_lam_im, ssm_log_dt, ssm_b_re, ssm_b_im, ssm_c_re, ssm_c_im, ssm_d, ssm_w_glu, ssm_b_glu, w_o_ssm, conv_w, conv_b, conv_ln_g, conv_ln_b, w_o_conv, w_out, norm_ffn, ffn_w1, ffn_w3, ffn_w2, moe_router, moe_w1, moe_w3, moe_w2, norm_final):
    batch, seq, d = x.shape
    depth = w_in.shape[0]
    assert d == D_MODEL and depth == 2 and seq % SSM_CHUNK == 0
    t = batch * seq
    xf = x.reshape(t, d)
    cos_t, sin_t = _rope_tables(positions)
    ng = depth * SSM_GROUPS
    klag, win, wout_s, apr, api = _ssm_tables(
        ssm_lam_re.reshape(ng, SSM_STATE), ssm_lam_im.reshape(ng, SSM_STATE), ssm_log_dt.reshape(ng),
        ssm_b_re.reshape(ng, SSM_STATE, SSM_GROUP), ssm_b_im.reshape(ng, SSM_STATE, SSM_GROUP),
        ssm_c_re.reshape(ng, SSM_GROUP, SSM_STATE), ssm_c_im.reshape(ng, SSM_GROUP, SSM_STATE))

    for layer in range(depth):
        moe_layer = layer % 2 == 1
        w = _prep_mixer_weights(layer, w_in, q_norm, w_uq, kv_norm, w_ukv, w_o_attn, ssm_w_glu, ssm_b_glu,
                                w_o_ssm, conv_w, conv_b, conv_ln_g, conv_ln_b, w_o_conv, w_out, norm_ffn)
        q, k, vt, ut, zc, gates = _mixer_in(xf, norm_mix[layer].reshape(1, d), w, cos_t, sin_t)
        yt = _attention(q, k, vt, batch, seq)
        st = _ssm(ut, klag, win, wout_s, apr, api, ssm_d[layer], layer, batch)
        i = layer // 2
        if not moe_layer:
            xn, h2 = _merge(yt, st, zc, gates, xf, w, seq, with_router=False)
            xf = _ffn(xn, h2, ffn_w1[i].astype(BF16), ffn_w3[i].astype(BF16), ffn_w2[i].astype(BF16))
        else:
            wr = jnp.pad(moe_router[i], ((0, 0), (0, LANES - N_EXPERTS)))
            w["wrh"] = wr.astype(BF16)
            w["wrl"] = (wr - w["wrh"].astype(F32)).astype(BF16)
            xn, h2, route = _merge(yt, st, zc, gates, xf, w, seq, with_router=True)
            dest, tile_expert, n_used, rows = _moe_plan(route, MOE_ROW_TILE)
            xs = _dispatch(h2, dest, rows)
            ys = _experts(xs, tile_expert, n_used, moe_w1[i].astype(BF16), moe_w3[i].astype(BF16),
                          moe_w2[i].astype(BF16))
            xf = _combine(xn, route, dest, ys, norm_final)
    return xf.reshape(batch, seq, d)
```
